```python
import math
import jax, jax.numpy as jnp
from jax import lax
import numpy as np

D_MODEL = 1024
BATCH = 8
SEQ = 8192
DEPTH = 1

N_META = 16
BLOCK = 128
META_PAD = BLOCK - N_META
SB_HEADS = 8
SB_HEAD_DIM = 64
SB_WIDTH = SB_HEADS * SB_HEAD_DIM
CONV_WIDTH = 512
CONV_KERNEL = 31
N_GROUPS = 4
EXPERTS_PER_GROUP = 4
N_EXPERTS = N_GROUPS * EXPERTS_PER_GROUP
TOP_K = 2
EXPERT_FF = 512
IN_WIDTH = 3 * SB_WIDTH + 2 * CONV_WIDTH + 2 * D_MODEL
EPS = 1e-6

kernel_name = "sb_conformer_gated_hmoe_block"


def rmsnorm(x, g):
    x32 = x.astype(jnp.float32)
    y = x32 * lax.rsqrt(jnp.mean(x32 * x32, axis=-1, keepdims=True) + EPS)
    return (y * g.astype(jnp.float32)).astype(x.dtype)


def layernorm(x, g, b):
    x32 = x.astype(jnp.float32)
    mu = jnp.mean(x32, axis=-1, keepdims=True)
    xc = x32 - mu
    var = jnp.mean(xc * xc, axis=-1, keepdims=True)
    y = xc * lax.rsqrt(var + EPS) * g.astype(jnp.float32) + b.astype(jnp.float32)
    return y.astype(x.dtype)


def stick_breaking_attention(q, k, v):
    b, h, lp, dh = q.shape
    nb = lp // BLOCK
    scale = 1.0 / math.sqrt(dh)
    key_pos = jnp.arange(lp)
    qb = q.reshape(b, h, nb, BLOCK, dh).transpose(2, 0, 1, 3, 4)

    def one_block(args):
        q_blk, i = args
        q_pos = i * BLOCK + jnp.arange(BLOCK)
        valid = (key_pos[None, :] < q_pos[:, None]) & (key_pos[None, :] >= META_PAD)
        z = jnp.einsum('bhqd,bhkd->bhqk', q_blk, k, preferred_element_type=jnp.float32) * scale
        log_stay = jnp.where(valid, jax.nn.log_sigmoid(-z), 0.0)
        log_after = lax.cumsum(log_stay, axis=3, reverse=True) - log_stay
        a = jnp.where(valid, jnp.exp(jax.nn.log_sigmoid(z) + log_after), 0.0)
        return jnp.einsum('bhqk,bhkd->bhqd', a.astype(v.dtype), v)

    out = lax.map(one_block, (qb, jnp.arange(nb)))
    return out.transpose(1, 2, 0, 3, 4).reshape(b, h, lp, dh)


def conformer_conv(a, gate, conv_w, conv_b, ln_g, ln_b, w_o):
    u = a * jax.nn.sigmoid(gate)
    y = lax.conv_general_dilated(
        u, conv_w[:, None, :].astype(u.dtype), window_strides=(1,),
        padding=[(CONV_KERNEL - 1, 0)], dimension_numbers=('NWC', 'WIO', 'NWC'),
        feature_group_count=CONV_WIDTH) + conv_b
    y = jax.nn.silu(layernorm(y, ln_g, ln_b))
    return y @ w_o


def hierarchical_moe(xf, w_rg, b_rg, w_re, b_re, w_gate, w_up, w_down):
    t = xf.shape[0]
    g_prob = jax.nn.softmax((xf @ w_rg).astype(jnp.float32) + b_rg.astype(jnp.float32), axis=-1)
    g_p, g_idx = lax.top_k(g_prob, 1)
    e_logits = ((xf @ w_re).astype(jnp.float32) + b_re.astype(jnp.float32)).reshape(
        t, N_GROUPS, EXPERTS_PER_GROUP)
    e_logits = jnp.take_along_axis(e_logits, g_idx[:, :, None], axis=1)[:, 0]
    e_prob = jax.nn.softmax(e_logits, axis=-1)
    e_p, e_idx = lax.top_k(e_prob, TOP_K)
    e_p = e_p / jnp.sum(e_p, axis=-1, keepdims=True)
    w_tok = g_p * e_p
    expert_id = g_idx * EXPERTS_PER_GROUP + e_idx
    combine = jnp.sum(jax.nn.one_hot(expert_id, N_EXPERTS, dtype=jnp.float32) * w_tok[..., None],
                      axis=1).astype(xf.dtype)
    y = jnp.zeros_like(xf)
    for e in range(N_EXPERTS):
        hdn = jax.nn.silu(xf @ w_gate[e]) * (xf @ w_up[e])
        y = y + combine[:, e:e + 1] * (hdn @ w_down[e])
    return y


def setup_inputs(seed: int = 0) -> dict:
    key = jax.random.key(seed)
    ks = jax.random.split(key, 24)
    f32 = jnp.float32
    nrm = lambda k, shape, s: jax.random.normal(k, shape, f32) * s
    return {
        "x": nrm(ks[0], (BATCH, SEQ, D_MODEL), 1.0),
        "meta": nrm(ks[1], (N_META, D_MODEL), 1.0),
        "norm_mix": 1.0 + nrm(ks[2], (DEPTH, D_MODEL), 0.02),
        "w_in": nrm(ks[3], (DEPTH, D_MODEL, IN_WIDTH), D_MODEL ** -0.5),
        "w_sb_o": nrm(ks[4], (DEPTH, SB_WIDTH, D_MODEL), SB_WIDTH ** -0.5),
        "conv_w": nrm(ks[5], (DEPTH, CONV_KERNEL, CONV_WIDTH), CONV_KERNEL ** -0.5),
        "conv_b": nrm(ks[6], (DEPTH, CONV_WIDTH), 0.02),
        "conv_ln_g": 1.0 + nrm(ks[7], (DEPTH, CONV_WIDTH), 0.02),
        "conv_ln_b": nrm(ks[8], (DEPTH, CONV_WIDTH), 0.02),
        "w_conv_o": nrm(ks[9], (DEPTH, CONV_WIDTH, D_MODEL), CONV_WIDTH ** -0.5),
        "w_out": nrm(ks[10], (DEPTH, D_MODEL, D_MODEL), D_MODEL ** -0.5),
        "norm_ffn": 1.0 + nrm(ks[11], (DEPTH, D_MODEL), 0.02),
        "w_router_group": nrm(ks[12], (DEPTH, D_MODEL, N_GROUPS), D_MODEL ** -0.5),
        "b_router_group": nrm(ks[13], (DEPTH, N_GROUPS), 0.01),
        "w_router_expert": nrm(ks[14], (DEPTH, D_MODEL, N_EXPERTS), D_MODEL ** -0.5),
        "b_router_expert": nrm(ks[15], (DEPTH, N_EXPERTS), 0.01),
        "w_gate": nrm(ks[16], (DEPTH, N_EXPERTS, D_MODEL, EXPERT_FF), D_MODEL ** -0.5),
        "w_up": nrm(ks[17], (DEPTH, N_EXPERTS, D_MODEL, EXPERT_FF), D_MODEL ** -0.5),
        "w_down": nrm(ks[18], (DEPTH, N_EXPERTS, EXPERT_FF, D_MODEL), EXPERT_FF ** -0.5),
        "norm_final": 1.0 + nrm(ks[19], (D_MODEL,), 0.02),
    }


def reference(x, meta, norm_mix, w_in, w_sb_o, conv_w, conv_b, conv_ln_g, conv_ln_b, w_conv_o,
              w_out, norm_ffn, w_router_group, b_router_group, w_router_expert, b_router_expert,
              w_gate, w_up, w_down, norm_final):
    b = x.shape[0]
    h = jnp.concatenate(
        [jnp.broadcast_to(meta[None].astype(x.dtype), (b, N_META, D_MODEL)), x], axis=1)
    L = h.shape[1]

    def to_heads(t):
        t = t.reshape(b, L, SB_HEADS, SB_HEAD_DIM)
        t = jnp.pad(t, ((0, 0), (META_PAD, 0), (0, 0), (0, 0)))
        return t.transpose(0, 2, 1, 3)

    for l in range(DEPTH):
        xn = rmsnorm(h, norm_mix[l])
        u = xn @ w_in[l]
        o0 = 0
        q = u[..., o0:o0 + SB_WIDTH]; o0 += SB_WIDTH
        k = u[..., o0:o0 + SB_WIDTH]; o0 += SB_WIDTH
        v = u[..., o0:o0 + SB_WIDTH]; o0 += SB_WIDTH
        glu_a = u[..., o0:o0 + CONV_WIDTH]; o0 += CONV_WIDTH
        glu_b = u[..., o0:o0 + CONV_WIDTH]; o0 += CONV_WIDTH
        gate_sb = u[..., o0:o0 + D_MODEL]; o0 += D_MODEL
        gate_conv = u[..., o0:o0 + D_MODEL]

        att = stick_breaking_attention(to_heads(q), to_heads(k), to_heads(v))[:, :, META_PAD:]
        y_sb = att.transpose(0, 2, 1, 3).reshape(b, L, SB_WIDTH) @ w_sb_o[l]
        y_conv = conformer_conv(glu_a, glu_b, conv_w[l], conv_b[l], conv_ln_g[l],
                                conv_ln_b[l], w_conv_o[l])
        merged = jax.nn.sigmoid(gate_sb) * y_sb + jax.nn.sigmoid(gate_conv) * y_conv
        h = h + merged @ w_out[l]

        xn2 = rmsnorm(h, norm_ffn[l]).reshape(b * L, D_MODEL)
        y_moe = hierarchical_moe(xn2, w_router_group[l], b_router_group[l], w_router_expert[l],
                                 b_router_expert[l], w_gate[l], w_up[l], w_down[l])
        h = h + y_moe.reshape(b, L, D_MODEL)

    return rmsnorm(h, norm_final)[:, N_META:]
```

```python
import functools

import jax
import jax.numpy as jnp
from jax import lax
from jax.experimental import pallas as pl
from jax.experimental.pallas import tpu as pltpu

N_META = 16
SB_HEAD_DIM = 64
SB_WIDTH = 512
CONV_WIDTH = 512
CONV_KERNEL = 31
N_GROUPS = 4
EXPERTS_PER_GROUP = 4
N_EXPERTS = N_GROUPS * EXPERTS_PER_GROUP
EXPERT_FF = 512
EPS = 1e-6

LANES = 128
HEAD_PAIR = 2 * SB_HEAD_DIM
CONV_HALO = 32
ROUTER_ROWS = 32
STAY_SUM_EXIT = 104.0
VMEM_LIMIT = 56 * 1024 * 1024

f32 = jnp.float32
bf16 = jnp.bfloat16


def _sigmoid(x):
    return 1.0 / (1.0 + jnp.exp(-x))


def _rms(x, g):
    return x * lax.rsqrt(jnp.mean(x * x, axis=-1, keepdims=True) + EPS) * g


def _const_spec(shape):
    return pl.BlockSpec(shape, lambda *_: (0,) * len(shape))


def _in_proj_kernel(x_ref, g_ref, w_ref, q_ref, k_ref, v_ref, u_ref, gs_ref, gc_ref):
    xb = _rms(x_ref[...], g_ref[...]).astype(bf16)

    def proj(lo, width):
        return jnp.dot(xb, w_ref[:, lo:lo + width], preferred_element_type=f32)

    o = 0
    q_ref[...] = (proj(o, SB_WIDTH) * (SB_HEAD_DIM ** -0.5)).astype(bf16); o += SB_WIDTH
    k_ref[...] = proj(o, SB_WIDTH).astype(bf16); o += SB_WIDTH
    v_ref[...] = proj(o, SB_WIDTH).astype(bf16); o += SB_WIDTH
    a = proj(o, CONV_WIDTH); o += CONV_WIDTH
    b = proj(o, CONV_WIDTH); o += CONV_WIDTH
    u_ref[...] = (a * _sigmoid(b)).astype(bf16)
    d = gs_ref.shape[1]
    gs_ref[...] = _sigmoid(proj(o, d)).astype(bf16); o += d
    gc_ref[...] = _sigmoid(proj(o, d)).astype(bf16)


def _in_proj(x2, g, w_b, tm):
    t, d = x2.shape
    n_in = w_b.shape[1]
    row = lambda w: pl.BlockSpec((tm, w), lambda i: (i, 0))
    sds = lambda w: jax.ShapeDtypeStruct((t, w), bf16)
    return pl.pallas_call(
        _in_proj_kernel,
        grid=(t // tm,),
        in_specs=[row(d), _const_spec((1, d)), _const_spec((d, n_in))],
        out_specs=[row(SB_WIDTH), row(SB_WIDTH), row(SB_WIDTH), row(CONV_WIDTH), row(d), row(d)],
        out_shape=[sds(SB_WIDTH), sds(SB_WIDTH), sds(SB_WIDTH), sds(CONV_WIDTH), sds(d), sds(d)],
        compiler_params=pltpu.CompilerParams(
            dimension_semantics=("arbitrary",), vmem_limit_bytes=VMEM_LIMIT),
        name="in_proj",
    )(x2, g, w_b)


def _attn_kernel(q_ref, k_ref, v_ref, km_ref, vm_ref, tri_ref, o_ref, acc_ref, c_ref, *, tq):
    i = pl.program_id(2)
    lane = lax.broadcasted_iota(jnp.int32, (1, LANES), 1)
    low = lane < SB_HEAD_DIM
    q2 = q_ref[0]
    zero = jnp.zeros_like(q2)
    q_heads = (jnp.where(low, q2, zero), jnp.where(low, zero, q2))

    def visit(kt, vt, valid):
        tk = kt.shape[0]
        tri = tri_ref[:tk, :tk]
        ones = jnp.ones((tk, LANES), bf16)
        pv = []
        for h in range(2):
            z = lax.dot_general(q_heads[h], kt, (((1,), (1,)), ((), ())),
                                preferred_element_type=f32)
            sp = jnp.maximum(z, 0.0) + jnp.log(1.0 + jnp.exp(-jnp.abs(z)))
            if valid is not None:
                sp = jnp.where(valid, sp, 0.0)
            spb = sp.astype(bf16)
            after = jnp.dot(spb, tri, preferred_element_type=f32)
            row_sum = jnp.dot(spb, ones, preferred_element_type=f32)
            c = c_ref[h]
            a = jnp.exp(z - sp - after - jnp.tile(c, (1, tk // LANES)))
            if valid is not None:
                a = jnp.where(valid, a, 0.0)
            pv.append(jnp.dot(a.astype(bf16), vt, preferred_element_type=f32))
            c_ref[h] = c + row_sum
        acc_ref[...] += jnp.where(low, pv[0], pv[1])

    def more_needed():
        return jnp.min(jnp.minimum(c_ref[0], c_ref[1])) < STAY_SUM_EXIT

    acc_ref[...] = jnp.zeros_like(acc_ref)
    c_ref[...] = jnp.zeros_like(c_ref)

    row = lax.broadcasted_iota(jnp.int32, (tq, tq), 0)
    col = lax.broadcasted_iota(jnp.int32, (tq, tq), 1)
    start = pl.multiple_of(i * tq, tq)
    visit(k_ref[0, pl.ds(start, tq), :], v_ref[0, pl.ds(start, tq), :], col < row)

    def cond(carry):
        j, more = carry
        return jnp.logical_and(j >= 0, more)

    def body(carry):
        j, _ = carry
        off = pl.multiple_of(j * tq, tq)
        visit(k_ref[0, pl.ds(off, tq), :], v_ref[0, pl.ds(off, tq), :], None)
        return j - 1, more_needed()

    _, more = lax.while_loop(cond, body, (i - 1, more_needed()))

    @pl.when(more)
    def _():
        mcol = lax.broadcasted_iota(jnp.int32, (tq, LANES), 1)
        visit(km_ref[...], vm_ref[...], mcol < N_META)

    o_ref[0] = acc_ref[...].astype(o_ref.dtype)


def _attention(q, k, v, km, vm, tq):
    b, s, w = q.shape
    n_pairs = w // HEAD_PAIR
    r = lax.broadcasted_iota(jnp.int32, (tq, tq), 0)
    c = lax.broadcasted_iota(jnp.int32, (tq, tq), 1)
    tri = (r > c).astype(bf16)
    seq = pl.BlockSpec((1, s, HEAD_PAIR), lambda bi, hp, i: (bi, 0, hp))
    tile = pl.BlockSpec((1, tq, HEAD_PAIR), lambda bi, hp, i: (bi, i, hp))
    meta = pl.BlockSpec((LANES, HEAD_PAIR), lambda bi, hp, i: (0, hp))
    return pl.pallas_call(
        functools.partial(_attn_kernel, tq=tq),
        grid=(b, n_pairs, s // tq),
        in_specs=[tile, seq, seq, meta, meta, _const_spec((tq, tq))],
        out_specs=tile,
        out_shape=jax.ShapeDtypeStruct((b, s, w), bf16),
        scratch_shapes=[pltpu.VMEM((tq, LANES), f32), pltpu.VMEM((2, tq, LANES), f32)],
        compiler_params=pltpu.CompilerParams(
            dimension_semantics=("arbitrary", "arbitrary", "arbitrary"),
            vmem_limit_bytes=VMEM_LIMIT),
        name="sb_attention",
    )(q, k, v, km, vm, tri)


def _mix_kernel(att_ref, u_ref, prev_ref, mhalo_ref, gs_ref, gc_ref, x_ref,
                wsb_ref, wcv_ref, wout_ref, cw_ref, cb_ref, lng_ref, lnb_ref, nf_ref,
                wrt_ref, brt_ref,
                h_ref, xn_ref, comb_ref,
                ubuf_ref, ycv_ref, *, tm, tiles_per_seq, chunk):
    i = pl.program_id(0)
    first = (i % tiles_per_seq) == 0
    halo = jnp.where(first, mhalo_ref[...], prev_ref[...])
    ubuf_ref[0:CONV_HALO, :] = halo.astype(f32)
    ubuf_ref[CONV_HALO:, :] = u_ref[...].astype(f32)

    lead = CONV_HALO - (CONV_KERNEL - 1)
    for r0 in range(0, tm, chunk):
        y = jnp.broadcast_to(cb_ref[...], (chunk, CONV_WIDTH))
        for kk in range(CONV_KERNEL):
            y = y + cw_ref[kk:kk + 1, :] * ubuf_ref[r0 + lead + kk:r0 + lead + kk + chunk, :]
        mu = jnp.mean(y, axis=-1, keepdims=True)
        yc = y - mu
        var = jnp.mean(yc * yc, axis=-1, keepdims=True)
        yn = yc * lax.rsqrt(var + EPS) * lng_ref[...] + lnb_ref[...]
        ycv_ref[r0:r0 + chunk, :] = (yn * _sigmoid(yn)).astype(bf16)

    y_sb = jnp.dot(att_ref[...], wsb_ref[...], preferred_element_type=f32)
    y_cv = jnp.dot(ycv_ref[...], wcv_ref[...], preferred_element_type=f32)
    merged = gs_ref[...].astype(f32) * y_sb + gc_ref[...].astype(f32) * y_cv
    h = x_ref[...] + jnp.dot(merged.astype(bf16), wout_ref[...], preferred_element_type=f32)
    h_ref[...] = h
    xnb = _rms(h, nf_ref[...]).astype(bf16)
    xn_ref[...] = xnb

    lt = lax.dot_general(wrt_ref[...], xnb, (((1,), (1,)), ((), ())),
                         preferred_element_type=f32) + brt_ref[...]
    gl = [lt[N_EXPERTS + g:N_EXPERTS + g + 1, :] for g in range(N_GROUPS)]
    gmax = functools.reduce(jnp.maximum, gl)
    gsum = functools.reduce(lambda a, b: a + b, [jnp.exp(x - gmax) for x in gl])
    g_p = 1.0 / gsum
    gidx = jnp.full_like(gmax, N_GROUPS - 1)
    for g in range(N_GROUPS - 2, -1, -1):
        gidx = jnp.where(gl[g] == gmax, float(g), gidx)
    el = []
    for j in range(EXPERTS_PER_GROUP):
        x = lt[(N_GROUPS - 1) * EXPERTS_PER_GROUP + j:(N_GROUPS - 1) * EXPERTS_PER_GROUP + j + 1, :]
        for g in range(N_GROUPS - 2, -1, -1):
            x = jnp.where(gidx == float(g), lt[g * EXPERTS_PER_GROUP + j:g * EXPERTS_PER_GROUP + j + 1, :], x)
        el.append(x)
    m1 = functools.reduce(jnp.maximum, el)
    i1 = jnp.full_like(m1, EXPERTS_PER_GROUP - 1)
    for j in range(EXPERTS_PER_GROUP - 2, -1, -1):
        i1 = jnp.where(el[j] == m1, float(j), i1)
    el2 = [jnp.where(i1 == float(j), -jnp.inf, el[j]) for j in range(EXPERTS_PER_GROUP)]
    m2 = functools.reduce(jnp.maximum, el2)
    i2 = jnp.full_like(m2, EXPERTS_PER_GROUP - 1)
    for j in range(EXPERTS_PER_GROUP - 2, -1, -1):
        i2 = jnp.where(el2[j] == m2, float(j), i2)
    w1 = g_p / (1.0 + jnp.exp(m2 - m1))
    w2 = g_p - w1
    e1 = gidx * EXPERTS_PER_GROUP + i1
    e2 = gidx * EXPERTS_PER_GROUP + i2
    eid = lax.broadcasted_iota(jnp.int32, (LANES, tm), 0).astype(f32)
    comb_t = jnp.where(eid == e1, w1, 0.0) + jnp.where(eid == e2, w2, 0.0)
    comb_ref[...] = comb_t.T


def _mix(att, u, mhalo, gs, gc, x2, wsb, wcv, wout, cw, cb, lng, lnb, nf, wrt, brt, tm, seq_len):
    t, d = x2.shape
    tiles_per_seq = seq_len // tm
    row = lambda w: pl.BlockSpec((tm, w), lambda i: (i, 0))
    halo_blocks = tm // CONV_HALO
    prev = pl.BlockSpec((CONV_HALO, CONV_WIDTH), lambda i: (jnp.maximum(i * halo_blocks - 1, 0), 0))
    kern = functools.partial(_mix_kernel, tm=tm, tiles_per_seq=tiles_per_seq, chunk=64)
    return pl.pallas_call(
        kern,
        grid=(t // tm,),
        in_specs=[row(SB_WIDTH), row(CONV_WIDTH), prev, _const_spec((CONV_HALO, CONV_WIDTH)),
                  row(d), row(d), row(d),
                  _const_spec((SB_WIDTH, d)), _const_spec((CONV_WIDTH, d)), _const_spec((d, d)),
                  _const_spec((CONV_KERNEL, CONV_WIDTH)), _const_spec((1, CONV_WIDTH)),
                  _const_spec((1, CONV_WIDTH)), _const_spec((1, CONV_WIDTH)), _const_spec((1, d)),
                  _const_spec((ROUTER_ROWS, d)), _const_spec((ROUTER_ROWS, 1))],
        out_specs=[row(d), row(d), row(LANES)],
        out_shape=[jax.ShapeDtypeStruct((t, d), f32), jax.ShapeDtypeStruct((t, d), bf16),
                   jax.ShapeDtypeStruct((t, LANES), f32)],
        scratch_shapes=[pltpu.VMEM((tm + CONV_HALO, CONV_WIDTH), f32),
                        pltpu.VMEM((tm, CONV_WIDTH), bf16)],
        compiler_params=pltpu.CompilerParams(
            dimension_semantics=("arbitrary",), vmem_limit_bytes=VMEM_LIMIT),
        name="mix",
    )(att, u, u, mhalo, gs, gc, x2, wsb, wcv, wout, cw, cb, lng, lnb, nf, wrt, brt)


def _moe_kernel(xn_ref, h_ref, comb_ref, wgu_ref, wd_ref, nf_ref, o_ref, y_ref):
    e = pl.program_id(1)

    @pl.when(e == 0)
    def _():
        y_ref[...] = jnp.zeros_like(y_ref)

    gu = jnp.dot(xn_ref[...], wgu_ref[0], preferred_element_type=f32)
    g = gu[:, :EXPERT_FF]
    hdn = (g * _sigmoid(g)) * gu[:, EXPERT_FF:]
    lane = lax.broadcasted_iota(jnp.int32, comb_ref.shape, 1)
    ce = jnp.sum(jnp.where(lane == e, comb_ref[...], 0.0), axis=-1, keepdims=True)
    y_ref[...] += jnp.dot((hdn * ce).astype(bf16), wd_ref[0], preferred_element_type=f32)

    @pl.when(e == pl.num_programs(1) - 1)
    def _():
        o_ref[...] = _rms(h_ref[...] + y_ref[...], nf_ref[...])


def _moe(xn2, h, comb, wgu, wd, nf, tm):
    t, d = h.shape
    n_e = wgu.shape[0]
    row = lambda w: pl.BlockSpec((tm, w), lambda i, e: (i, 0))
    return pl.pallas_call(
        _moe_kernel,
        grid=(t // tm, n_e),
        in_specs=[row(d), row(d), row(LANES),
                  pl.BlockSpec((1, d, 2 * EXPERT_FF), lambda i, e: (e, 0, 0)),
                  pl.BlockSpec((1, EXPERT_FF, d), lambda i, e: (e, 0, 0)),
                  pl.BlockSpec((1, d), lambda i, e: (0, 0))],
        out_specs=row(d),
        out_shape=jax.ShapeDtypeStruct((t, d), f32),
        scratch_shapes=[pltpu.VMEM((tm, d), f32)],
        compiler_params=pltpu.CompilerParams(
            dimension_semantics=("arbitrary", "arbitrary"), vmem_limit_bytes=VMEM_LIMIT),
        name="moe",
    )(xn2, h, comb, wgu, wd, nf)


def kernel(x, meta, norm_mix, w_in, w_sb_o, conv_w, conv_b, conv_ln_g, conv_ln_b, w_conv_o, w_out,
           norm_ffn, w_router_group, b_router_group, w_router_expert, b_router_expert, w_gate, w_up,
           w_down, norm_final):
    assert norm_mix.shape[0] == 1, "single-layer block"
    b, s, d = x.shape
    t = b * s
    tm = min(512, s)
    tq = min(256, s)
    x2 = x.reshape(t, d)
    w_in_b = w_in[0].astype(bf16)
    g_mix = norm_mix[0][None]

    q, k, v, u, gs, gc = _in_proj(x2, g_mix, w_in_b, tm)
    _, km, vm, um, _, _ = _in_proj(meta, g_mix, w_in_b, N_META)
    pad_rows = lambda a, top, bottom: jnp.pad(a, ((top, bottom), (0, 0)))
    km = pad_rows(km, 0, LANES - N_META)
    vm = pad_rows(vm, 0, LANES - N_META)
    shp = (b, s, SB_WIDTH)
    att = _attention(q.reshape(shp), k.reshape(shp), v.reshape(shp), km, vm, tq).reshape(t, SB_WIDTH)

    mhalo = pad_rows(um, CONV_HALO - N_META, 0)
    wrt = jnp.concatenate([w_router_expert[0], w_router_group[0]], axis=1).T
    wrt = pad_rows(wrt, 0, ROUTER_ROWS - wrt.shape[0]).astype(bf16)
    brt = jnp.concatenate([b_router_expert[0], b_router_group[0]])[:, None]
    brt = pad_rows(brt, 0, ROUTER_ROWS - brt.shape[0]).astype(f32)
    h, xn2, comb = _mix(att, u, mhalo, gs, gc, x2,
                        w_sb_o[0].astype(bf16), w_conv_o[0].astype(bf16), w_out[0].astype(bf16),
                        conv_w[0], conv_b[0][None], conv_ln_g[0][None], conv_ln_b[0][None],
                        norm_ffn[0][None], wrt, brt, tm, s)

    wgu = jnp.concatenate([w_gate[0], w_up[0]], axis=-1).astype(bf16)
    out = _moe(xn2, h, comb, wgu, w_down[0].astype(bf16), norm_final[None], min(1024, s))
    return out.reshape(b, s, d)
```

```python
import functools

import jax
import jax.numpy as jnp
from jax import lax
from jax.experimental import pallas as pl
from jax.experimental.pallas import tpu as pltpu

N_META = 16
SB_HEAD_DIM = 64
SB_WIDTH = 512
CONV_WIDTH = 512
CONV_KERNEL = 31
N_GROUPS = 4
EXPERTS_PER_GROUP = 4
N_EXPERTS = N_GROUPS * EXPERTS_PER_GROUP
EXPERT_FF = 512
EPS = 1e-6

LANES = 128
HEAD_PAIR = 2 * SB_HEAD_DIM
CONV_HALO = 32
ROUTER_ROWS = 32
STAY_SUM_EXIT = 104.0
VMEM_LIMIT = 56 * 1024 * 1024

f32 = jnp.float32
bf16 = jnp.bfloat16
u32 = jnp.uint32


def _sigmoid(x):
    return 1.0 / (1.0 + jnp.exp(-x))


def _rms(x, g):
    return x * lax.rsqrt(jnp.mean(x * x, axis=-1, keepdims=True) + EPS) * g


def _const_spec(shape):
    return pl.BlockSpec(shape, lambda *_: (0,) * len(shape))


def _pack_halves(x):
    w = x.shape[1] // 2
    hi = lax.bitcast_convert_type(x[:, :w].astype(bf16).astype(f32), u32)
    lo = lax.bitcast_convert_type(x[:, w:].astype(bf16).astype(f32), u32)
    return hi | (lo >> 16)


def _unpack_halves(p):
    hi = lax.bitcast_convert_type(p & jnp.uint32(0xFFFF0000), f32)
    lo = lax.bitcast_convert_type(p << 16, f32)
    return hi.astype(bf16), lo.astype(bf16)


def _in_proj_kernel(x_ref, g_ref, w_ref, q_ref, k_ref, v_ref, u_ref, gs_ref, gc_ref):
    xb = _rms(x_ref[...], g_ref[...]).astype(bf16)

    def proj(lo, width):
        return jnp.dot(xb, w_ref[:, lo:lo + width], preferred_element_type=f32)

    o = 0
    q_ref[...] = (proj(o, SB_WIDTH) * (SB_HEAD_DIM ** -0.5)).astype(bf16); o += SB_WIDTH
    k_ref[...] = proj(o, SB_WIDTH).astype(bf16); o += SB_WIDTH
    v_ref[...] = proj(o, SB_WIDTH).astype(bf16); o += SB_WIDTH
    a = proj(o, CONV_WIDTH); o += CONV_WIDTH
    b = proj(o, CONV_WIDTH); o += CONV_WIDTH
    u_ref[...] = (a * _sigmoid(b)).astype(bf16)
    d = gs_ref.shape[1]
    gs_ref[...] = _sigmoid(proj(o, d)).astype(bf16); o += d
    gc_ref[...] = _sigmoid(proj(o, d)).astype(bf16)


def _in_proj(x2, g, w_b, tm):
    t, d = x2.shape
    n_in = w_b.shape[1]
    row = lambda w: pl.BlockSpec((tm, w), lambda i: (i, 0))
    sds = lambda w: jax.ShapeDtypeStruct((t, w), bf16)
    return pl.pallas_call(
        _in_proj_kernel,
        grid=(t // tm,),
        in_specs=[row(d), _const_spec((1, d)), _const_spec((d, n_in))],
        out_specs=[row(SB_WIDTH), row(SB_WIDTH), row(SB_WIDTH), row(CONV_WIDTH), row(d), row(d)],
        out_shape=[sds(SB_WIDTH), sds(SB_WIDTH), sds(SB_WIDTH), sds(CONV_WIDTH), sds(d), sds(d)],
        compiler_params=pltpu.CompilerParams(
            dimension_semantics=("arbitrary",), vmem_limit_bytes=VMEM_LIMIT),
        name="in_proj",
    )(x2, g, w_b)


def _attn_kernel(q_ref, k_ref, v_ref, km_ref, vm_ref, tri_ref, o_ref, acc_ref, c_ref, *, tq):
    i = pl.program_id(2)
    lane = lax.broadcasted_iota(jnp.int32, (1, LANES), 1)
    low = lane < SB_HEAD_DIM
    q2 = q_ref[0]
    zero = jnp.zeros_like(q2)
    q_heads = (jnp.where(low, q2, zero), jnp.where(low, zero, q2))

    def visit(kt, vt, valid):
        tk = kt.shape[0]
        tri = tri_ref[:tk, :tk]
        ones = jnp.ones((tk, LANES), bf16)
        pv = []
        for h in range(2):
            z = lax.dot_general(q_heads[h], kt, (((1,), (1,)), ((), ())),
                                preferred_element_type=f32)
            sp = jnp.maximum(z, 0.0) + jnp.log(1.0 + jnp.exp(-jnp.abs(z)))
            if valid is not None:
                sp = jnp.where(valid, sp, 0.0)
            spb = sp.astype(bf16)
            after = jnp.dot(spb, tri, preferred_element_type=f32)
            row_sum = jnp.dot(spb, ones, preferred_element_type=f32)
            c = c_ref[h]
            a = jnp.exp(z - sp - after - jnp.tile(c, (1, tk // LANES)))
            if valid is not None:
                a = jnp.where(valid, a, 0.0)
            pv.append(jnp.dot(a.astype(bf16), vt, preferred_element_type=f32))
            c_ref[h] = c + row_sum
        acc_ref[...] += jnp.where(low, pv[0], pv[1])

    def more_needed():
        return jnp.min(jnp.minimum(c_ref[0], c_ref[1])) < STAY_SUM_EXIT

    acc_ref[...] = jnp.zeros_like(acc_ref)
    c_ref[...] = jnp.zeros_like(c_ref)

    row = lax.broadcasted_iota(jnp.int32, (tq, tq), 0)
    col = lax.broadcasted_iota(jnp.int32, (tq, tq), 1)
    start = pl.multiple_of(i * tq, tq)
    visit(k_ref[0, pl.ds(start, tq), :], v_ref[0, pl.ds(start, tq), :], col < row)

    def cond(carry):
        j, more = carry
        return jnp.logical_and(j >= 0, more)

    def body(carry):
        j, _ = carry
        off = pl.multiple_of(j * tq, tq)
        visit(k_ref[0, pl.ds(off, tq), :], v_ref[0, pl.ds(off, tq), :], None)
        return j - 1, more_needed()

    _, more = lax.while_loop(cond, body, (i - 1, more_needed()))

    @pl.when(more)
    def _():
        mcol = lax.broadcasted_iota(jnp.int32, (tq, LANES), 1)
        visit(km_ref[...], vm_ref[...], mcol < N_META)

    o_ref[0] = acc_ref[...].astype(o_ref.dtype)


def _attention(q, k, v, km, vm, tq):
    b, s, w = q.shape
    n_pairs = w // HEAD_PAIR
    r = lax.broadcasted_iota(jnp.int32, (tq, tq), 0)
    c = lax.broadcasted_iota(jnp.int32, (tq, tq), 1)
    tri = (r > c).astype(bf16)
    seq = pl.BlockSpec((1, s, HEAD_PAIR), lambda bi, hp, i: (bi, 0, hp))
    tile = pl.BlockSpec((1, tq, HEAD_PAIR), lambda bi, hp, i: (bi, i, hp))
    meta = pl.BlockSpec((LANES, HEAD_PAIR), lambda bi, hp, i: (0, hp))
    return pl.pallas_call(
        functools.partial(_attn_kernel, tq=tq),
        grid=(b, n_pairs, s // tq),
        in_specs=[tile, seq, seq, meta, meta, _const_spec((tq, tq))],
        out_specs=tile,
        out_shape=jax.ShapeDtypeStruct((b, s, w), bf16),
        scratch_shapes=[pltpu.VMEM((tq, LANES), f32), pltpu.VMEM((2, tq, LANES), f32)],
        compiler_params=pltpu.CompilerParams(
            dimension_semantics=("arbitrary", "arbitrary", "arbitrary"),
            vmem_limit_bytes=VMEM_LIMIT),
        name="sb_attention",
    )(q, k, v, km, vm, tri)


def _mix_kernel(att_ref, u_ref, prev_ref, mhalo_ref, gs_ref, gc_ref, x_ref,
                wsb_ref, wcv_ref, wout_ref, cw_ref, cb_ref, lng_ref, lnb_ref, nf_ref,
                wrt_ref, brt_ref, upper_ref,
                h_ref, xn_ref, route_ref, wtok_ref, cnt_out_ref,
                ubuf_ref, ycv_ref, cnt_ref, *, tm, tiles_per_seq, chunk):
    i = pl.program_id(0)
    first = (i % tiles_per_seq) == 0
    halo = jnp.where(first, mhalo_ref[...], prev_ref[...])
    ubuf_ref[0:CONV_HALO, :] = halo.astype(f32)
    ubuf_ref[CONV_HALO:, :] = u_ref[...].astype(f32)

    lead = CONV_HALO - (CONV_KERNEL - 1)
    for r0 in range(0, tm, chunk):
        y = jnp.broadcast_to(cb_ref[...], (chunk, CONV_WIDTH))
        for kk in range(CONV_KERNEL):
            y = y + cw_ref[kk:kk + 1, :] * ubuf_ref[r0 + lead + kk:r0 + lead + kk + chunk, :]
        mu = jnp.mean(y, axis=-1, keepdims=True)
        yc = y - mu
        var = jnp.mean(yc * yc, axis=-1, keepdims=True)
        yn = yc * lax.rsqrt(var + EPS) * lng_ref[...] + lnb_ref[...]
        ycv_ref[r0:r0 + chunk, :] = (yn * _sigmoid(yn)).astype(bf16)

    y_sb = jnp.dot(att_ref[...], wsb_ref[...], preferred_element_type=f32)
    y_cv = jnp.dot(ycv_ref[...], wcv_ref[...], preferred_element_type=f32)
    merged = gs_ref[...].astype(f32) * y_sb + gc_ref[...].astype(f32) * y_cv
    h = x_ref[...] + jnp.dot(merged.astype(bf16), wout_ref[...], preferred_element_type=f32)
    h_ref[...] = h
    xn = _rms(h, nf_ref[...])
    xnb = xn.astype(bf16)
    xn_ref[...] = _pack_halves(xn)

    lt = lax.dot_general(wrt_ref[...], xnb, (((1,), (1,)), ((), ())),
                         preferred_element_type=f32) + brt_ref[...]
    gl = [lt[N_EXPERTS + g:N_EXPERTS + g + 1, :] for g in range(N_GROUPS)]
    gmax = functools.reduce(jnp.maximum, gl)
    gsum = functools.reduce(lambda a, b: a + b, [jnp.exp(x - gmax) for x in gl])
    g_p = 1.0 / gsum
    gidx = jnp.full_like(gmax, N_GROUPS - 1)
    for g in range(N_GROUPS - 2, -1, -1):
        gidx = jnp.where(gl[g] == gmax, float(g), gidx)
    el = []
    for j in range(EXPERTS_PER_GROUP):
        x = lt[(N_GROUPS - 1) * EXPERTS_PER_GROUP + j:(N_GROUPS - 1) * EXPERTS_PER_GROUP + j + 1, :]
        for g in range(N_GROUPS - 2, -1, -1):
            x = jnp.where(gidx == float(g), lt[g * EXPERTS_PER_GROUP + j:g * EXPERTS_PER_GROUP + j + 1, :], x)
        el.append(x)
    m1 = functools.reduce(jnp.maximum, el)
    i1 = jnp.full_like(m1, EXPERTS_PER_GROUP - 1)
    for j in range(EXPERTS_PER_GROUP - 2, -1, -1):
        i1 = jnp.where(el[j] == m1, float(j), i1)
    el2 = [jnp.where(i1 == float(j), -jnp.inf, el[j]) for j in range(EXPERTS_PER_GROUP)]
    m2 = functools.reduce(jnp.maximum, el2)
    i2 = jnp.full_like(m2, EXPERTS_PER_GROUP - 1)
    for j in range(EXPERTS_PER_GROUP - 2, -1, -1):
        i2 = jnp.where(el2[j] == m2, float(j), i2)
    w1 = g_p / (1.0 + jnp.exp(m2 - m1))
    w2 = g_p - w1
    e1 = gidx * EXPERTS_PER_GROUP + i1
    e2 = gidx * EXPERTS_PER_GROUP + i2

    @pl.when(i == 0)
    def _():
        cnt_ref[...] = jnp.zeros_like(cnt_ref)

    eid = lax.broadcasted_iota(jnp.int32, (N_EXPERTS, tm), 0).astype(f32)
    is1 = eid == e1
    is2 = eid == e2
    member = jnp.where(jnp.logical_or(is1, is2), 1.0, 0.0).astype(bf16)
    before = jnp.dot(member, upper_ref[...], preferred_element_type=f32)
    cnt = cnt_ref[...]
    rank = before + jnp.tile(cnt, (1, tm // LANES))
    cnt_ref[...] = cnt + jnp.dot(member, jnp.ones((tm, LANES), bf16), preferred_element_type=f32)
    cnt_out_ref[...] = cnt_ref[...]
    r1 = jnp.sum(jnp.where(is1, rank, 0.0), axis=0, keepdims=True)
    r2 = jnp.sum(jnp.where(is2, rank, 0.0), axis=0, keepdims=True)
    pad = jnp.zeros((4, tm), f32)
    route_ref[...] = jnp.concatenate([e1, e2, r1, r2, pad], axis=0).astype(jnp.int32)
    wrow = lax.broadcasted_iota(jnp.int32, (LANES, tm), 0)
    w_t = jnp.where(wrow == 0, w1, jnp.where(wrow == 1, w2, 0.0))
    wtok_ref[...] = w_t.T


def _mix(att, u, mhalo, gs, gc, x2, wsb, wcv, wout, cw, cb, lng, lnb, nf, wrt, brt, tm, seq_len):
    t, d = x2.shape
    tiles_per_seq = seq_len // tm
    row = lambda w: pl.BlockSpec((tm, w), lambda i: (i, 0))
    halo_blocks = tm // CONV_HALO
    prev = pl.BlockSpec((CONV_HALO, CONV_WIDTH), lambda i: (jnp.maximum(i * halo_blocks - 1, 0), 0))
    kern = functools.partial(_mix_kernel, tm=tm, tiles_per_seq=tiles_per_seq, chunk=64)
    r = lax.broadcasted_iota(jnp.int32, (tm, tm), 0)
    c = lax.broadcasted_iota(jnp.int32, (tm, tm), 1)
    upper = (r < c).astype(bf16)
    return pl.pallas_call(
        kern,
        grid=(t // tm,),
        in_specs=[row(SB_WIDTH), row(CONV_WIDTH), prev, _const_spec((CONV_HALO, CONV_WIDTH)),
                  row(d), row(d), row(d),
                  _const_spec((SB_WIDTH, d)), _const_spec((CONV_WIDTH, d)), _const_spec((d, d)),
                  _const_spec((CONV_KERNEL, CONV_WIDTH)), _const_spec((1, CONV_WIDTH)),
                  _const_spec((1, CONV_WIDTH)), _const_spec((1, CONV_WIDTH)), _const_spec((1, d)),
                  _const_spec((ROUTER_ROWS, d)), _const_spec((ROUTER_ROWS, 1)), _const_spec((tm, tm))],
        out_specs=[row(d), row(d // 2), pl.BlockSpec((8, tm), lambda i: (0, i)), row(LANES),
                   _const_spec((N_EXPERTS, LANES))],
        out_shape=[jax.ShapeDtypeStruct((t, d), f32), jax.ShapeDtypeStruct((t, d // 2), u32),
                   jax.ShapeDtypeStruct((8, t), jnp.int32), jax.ShapeDtypeStruct((t, LANES), f32),
                   jax.ShapeDtypeStruct((N_EXPERTS, LANES), f32)],
        scratch_shapes=[pltpu.VMEM((tm + CONV_HALO, CONV_WIDTH), f32),
                        pltpu.VMEM((tm, CONV_WIDTH), bf16),
                        pltpu.VMEM((N_EXPERTS, LANES), f32)],
        compiler_params=pltpu.CompilerParams(
            dimension_semantics=("arbitrary",), vmem_limit_bytes=VMEM_LIMIT),
        name="mix",
    )(att, u, u, mhalo, gs, gc, x2, wsb, wcv, wout, cw, cb, lng, lnb, nf, wrt, brt, upper)


def _row_copy(src, src_row, dst, dst_row, sem):
    return pltpu.make_async_copy(src.at[pl.ds(src_row, 1)], dst.at[pl.ds(dst_row, 1)], sem)


def _scatter_kernel(pos_ref, x_hbm, zeros_hbm, xs_hbm, sems, *, tm):
    del zeros_hbm
    i = pl.program_id(0)
    slot = i % 2

    def issue(r, carry):
        for c in range(2):
            _row_copy(x_hbm, i * tm + r, xs_hbm, pos_ref[0, c, r], sems.at[slot]).start()
        return carry

    lax.fori_loop(0, tm, issue, 0, unroll=8)

    def wait_tile(sl):
        pltpu.make_async_copy(x_hbm.at[pl.ds(0, 2 * tm)], xs_hbm.at[pl.ds(0, 2 * tm)],
                              sems.at[sl]).wait()

    @pl.when(i > 0)
    def _():
        wait_tile(1 - slot)

    @pl.when(i == pl.num_programs(0) - 1)
    def _():
        wait_tile(slot)


def _scatter(pos3, xpk, n_rows):
    n_tiles, _, tm = pos3.shape
    w = xpk.shape[1]
    any_spec = pl.BlockSpec(memory_space=pl.ANY)
    return pl.pallas_call(
        functools.partial(_scatter_kernel, tm=tm),
        grid=(n_tiles,),
        in_specs=[pl.BlockSpec((1, 2, tm), lambda i: (i, 0, 0), memory_space=pltpu.SMEM),
                  any_spec, any_spec],
        out_specs=any_spec,
        out_shape=jax.ShapeDtypeStruct((n_rows, w), u32),
        scratch_shapes=[pltpu.SemaphoreType.DMA((2,))],
        input_output_aliases={2: 0},
        compiler_params=pltpu.CompilerParams(dimension_semantics=("arbitrary",)),
        name="moe_scatter",
    )(pos3, xpk, jnp.zeros((n_rows, w), u32))


def _experts_kernel(te_ref, nu_ref, xs_ref, wgu_ref, wd_ref, ys_ref):
    del te_ref
    i = pl.program_id(0)

    @pl.when(i < nu_ref[0])
    def _():
        xa, xb = _unpack_halves(xs_ref[...])
        half = xa.shape[1]
        gu = (jnp.dot(xa, wgu_ref[0, :half, :], preferred_element_type=f32)
              + jnp.dot(xb, wgu_ref[0, half:, :], preferred_element_type=f32))
        g = gu[:, :EXPERT_FF]
        hdn = (g * _sigmoid(g)) * gu[:, EXPERT_FF:]
        ys_ref[...] = _pack_halves(jnp.dot(hdn.astype(bf16), wd_ref[0], preferred_element_type=f32))

    @pl.when(i >= nu_ref[0])
    def _():
        ys_ref[...] = jnp.zeros_like(ys_ref)


def _experts(tile_expert, n_used, xs, wgu, wd, tm):
    n_rows, w = xs.shape
    d = wd.shape[2]
    grid_spec = pltpu.PrefetchScalarGridSpec(
        num_scalar_prefetch=2,
        grid=(n_rows // tm,),
        in_specs=[pl.BlockSpec((tm, w), lambda i, te, nu: (i, 0)),
                  pl.BlockSpec((1, d, 2 * EXPERT_FF), lambda i, te, nu: (te[i], 0, 0)),
                  pl.BlockSpec((1, EXPERT_FF, d), lambda i, te, nu: (te[i], 0, 0))],
        out_specs=pl.BlockSpec((tm, w), lambda i, te, nu: (i, 0)),
    )
    return pl.pallas_call(
        _experts_kernel,
        grid_spec=grid_spec,
        out_shape=jax.ShapeDtypeStruct((n_rows, w), u32),
        compiler_params=pltpu.CompilerParams(
            dimension_semantics=("arbitrary",), vmem_limit_bytes=VMEM_LIMIT),
        name="moe_experts",
    )(tile_expert, n_used, xs, wgu, wd)


def _combine_kernel(pos_ref, nxt_ref, ys_hbm, wtok_ref, h_ref, nf_ref, o_ref, gbuf, sems, *, tm):
    i = pl.program_id(0)
    slot = i % 2

    def issue(p_ref, sl):
        def body(r, carry):
            for c in range(2):
                _row_copy(ys_hbm, p_ref[0, c, r], gbuf.at[sl, c], r, sems.at[sl]).start()
            return carry
        lax.fori_loop(0, tm, body, 0, unroll=8)

    @pl.when(i == 0)
    def _():
        issue(pos_ref, 0)

    @pl.when(i + 1 < pl.num_programs(0))
    def _():
        issue(nxt_ref, 1 - slot)

    for c in range(2):
        pltpu.make_async_copy(ys_hbm.at[pl.ds(0, tm)], gbuf.at[slot, c], sems.at[slot]).wait()

    wt = wtok_ref[...]
    y = None
    for c in range(2):
        ya, yb = _unpack_halves(gbuf[slot, c])
        yc = jnp.concatenate([ya, yb], axis=1).astype(f32) * wt[:, c:c + 1]
        y = yc if y is None else y + yc
    o_ref[...] = _rms(h_ref[...] + y, nf_ref[...])


def _combine(pos3, ys, wtok, h, nf):
    n_tiles, _, tm = pos3.shape
    t, d = h.shape
    w = ys.shape[1]
    row = lambda wd_: pl.BlockSpec((tm, wd_), lambda i: (i, 0))
    last = n_tiles - 1
    return pl.pallas_call(
        functools.partial(_combine_kernel, tm=tm),
        grid=(n_tiles,),
        in_specs=[pl.BlockSpec((1, 2, tm), lambda i: (i, 0, 0), memory_space=pltpu.SMEM),
                  pl.BlockSpec((1, 2, tm), lambda i: (jnp.minimum(i + 1, last), 0, 0),
                               memory_space=pltpu.SMEM),
                  pl.BlockSpec(memory_space=pl.ANY), row(LANES), row(d), _const_spec((1, d))],
        out_specs=row(d),
        out_shape=jax.ShapeDtypeStruct((t, d), f32),
        scratch_shapes=[pltpu.VMEM((2, 2, tm, w), u32), pltpu.SemaphoreType.DMA((2,))],
        compiler_params=pltpu.CompilerParams(
            dimension_semantics=("arbitrary",), vmem_limit_bytes=VMEM_LIMIT),
        name="moe_combine",
    )(pos3, pos3, ys, wtok, h, nf)


def _tile_positions(pos, tm):
    return pos.reshape(2, -1, tm).transpose(1, 0, 2)


def kernel(x, meta, norm_mix, w_in, w_sb_o, conv_w, conv_b, conv_ln_g, conv_ln_b, w_conv_o, w_out,
           norm_ffn, w_router_group, b_router_group, w_router_expert, b_router_expert, w_gate, w_up,
           w_down, norm_final):
    assert norm_mix.shape[0] == 1, "single-layer block"
    b, s, d = x.shape
    t = b * s
    tm = min(512, s)
    tq = min(256, s)
    x2 = x.reshape(t, d)
    w_in_b = w_in[0].astype(bf16)
    g_mix = norm_mix[0][None]

    q, k, v, u, gs, gc = _in_proj(x2, g_mix, w_in_b, tm)
    _, km, vm, um, _, _ = _in_proj(meta, g_mix, w_in_b, N_META)
    pad_rows = lambda a, top, bottom: jnp.pad(a, ((top, bottom), (0, 0)))
    km = pad_rows(km, 0, LANES - N_META)
    vm = pad_rows(vm, 0, LANES - N_META)
    shp = (b, s, SB_WIDTH)
    att = _attention(q.reshape(shp), k.reshape(shp), v.reshape(shp), km, vm, tq).reshape(t, SB_WIDTH)

    mhalo = pad_rows(um, CONV_HALO - N_META, 0)
    wrt = jnp.concatenate([w_router_expert[0], w_router_group[0]], axis=1).T
    wrt = pad_rows(wrt, 0, ROUTER_ROWS - wrt.shape[0]).astype(bf16)
    brt = jnp.concatenate([b_router_expert[0], b_router_group[0]])[:, None]
    brt = pad_rows(brt, 0, ROUTER_ROWS - brt.shape[0]).astype(f32)
    h, xpk, route, wtok, counts = _mix(
        att, u, mhalo, gs, gc, x2,
        w_sb_o[0].astype(bf16), w_conv_o[0].astype(bf16), w_out[0].astype(bf16),
        conv_w[0], conv_b[0][None], conv_ln_g[0][None], conv_ln_b[0][None],
        norm_ffn[0][None], wrt, brt, tm, s)

    te = min(512, s)
    n_rows = 2 * t + N_EXPERTS * te
    cnt = counts[:, 0].astype(jnp.int32)
    tiles = (cnt + te - 1) // te
    tile_end = jnp.cumsum(tiles)
    row_start = (tile_end - tiles) * te
    pos = row_start[route[0:2]] + route[2:4]
    tile_expert = jnp.minimum(
        jnp.searchsorted(tile_end, jnp.arange(n_rows // te, dtype=jnp.int32), side="right"),
        N_EXPERTS - 1).astype(jnp.int32)
    n_used = tile_end[-1:].astype(jnp.int32)

    xs = _scatter(_tile_positions(pos, min(1024, s)), xpk, n_rows)
    wgu = jnp.concatenate([w_gate[0], w_up[0]], axis=-1).astype(bf16)
    ys = _experts(tile_expert, n_used, xs, wgu, w_down[0].astype(bf16), te)
    out = _combine(_tile_positions(pos, min(256, s)), ys, wtok, h, norm_final[None])
    return out.reshape(b, s, d)
```

```python
import functools

import jax
import jax.numpy as jnp
from jax import lax
from jax.experimental import pallas as pl
from jax.experimental.pallas import tpu as pltpu

N_META = 16
SB_HEAD_DIM = 64
SB_WIDTH = 512
CONV_WIDTH = 512
CONV_KERNEL = 31
N_GROUPS = 4
EXPERTS_PER_GROUP = 4
N_EXPERTS = N_GROUPS * EXPERTS_PER_GROUP
EXPERT_FF = 512
EPS = 1e-6

LANES = 128
SUBLANES = 8
HEAD_PAIR = 2 * SB_HEAD_DIM
CONV_HALO = 32
ROUTER_ROWS = 32
STAY_SUM_EXIT = 150.0
LOG2_E = 1.4426950408889634
VMEM_LIMIT = 56 * 1024 * 1024

f32 = jnp.float32
bf16 = jnp.bfloat16


def _sigmoid(x):
    return 1.0 / (1.0 + jnp.exp(-x))


def _rms(x, g):
    return x * lax.rsqrt(jnp.mean(x * x, axis=-1, keepdims=True) + EPS) * g


def _const_spec(shape):
    return pl.BlockSpec(shape, lambda *_: (0,) * len(shape))


def _in_proj_kernel(x_ref, g_ref, w_ref, q_ref, k_ref, v_ref, u_ref, gs_ref, gc_ref):
    xb = _rms(x_ref[...], g_ref[...]).astype(bf16)

    def proj(lo, width):
        return jnp.dot(xb, w_ref[:, lo:lo + width], preferred_element_type=f32)

    o = 0
    q_ref[...] = (proj(o, SB_WIDTH) * (SB_HEAD_DIM ** -0.5 * LOG2_E)).astype(bf16); o += SB_WIDTH
    k_ref[...] = proj(o, SB_WIDTH).astype(bf16); o += SB_WIDTH
    v_ref[...] = proj(o, SB_WIDTH).astype(bf16); o += SB_WIDTH
    a = proj(o, CONV_WIDTH); o += CONV_WIDTH
    b = proj(o, CONV_WIDTH); o += CONV_WIDTH
    u_ref[...] = (a * _sigmoid(b)).astype(bf16)
    d = gs_ref.shape[1]
    gs_ref[...] = _sigmoid(proj(o, d)).astype(bf16); o += d
    gc_ref[...] = _sigmoid(proj(o, d)).astype(bf16)


def _in_proj(x2, g, w_b, tm):
    t, d = x2.shape
    n_in = w_b.shape[1]
    row = lambda w: pl.BlockSpec((tm, w), lambda i: (i, 0))
    sds = lambda w: jax.ShapeDtypeStruct((t, w), bf16)
    return pl.pallas_call(
        _in_proj_kernel,
        grid=(t // tm,),
        in_specs=[row(d), _const_spec((1, d)), _const_spec((d, n_in))],
        out_specs=[row(SB_WIDTH), row(SB_WIDTH), row(SB_WIDTH), row(CONV_WIDTH), row(d), row(d)],
        out_shape=[sds(SB_WIDTH), sds(SB_WIDTH), sds(SB_WIDTH), sds(CONV_WIDTH), sds(d), sds(d)],
        compiler_params=pltpu.CompilerParams(
            dimension_semantics=("arbitrary",), vmem_limit_bytes=VMEM_LIMIT),
        name="in_proj",
    )(x2, g, w_b)


def _attn_kernel(q_ref, k_ref, v_ref, km_ref, vm_ref, tri_ref, o_ref, acc_ref, c_ref, *, tq, n_pairs):
    i = pl.program_id(1)
    lane = lax.broadcasted_iota(jnp.int32, (1, LANES), 1)
    low = lane < SB_HEAD_DIM

    def visit(load_kv, valid):
        for p in range(n_pairs):
            cols = slice(p * HEAD_PAIR, (p + 1) * HEAD_PAIR)
            kt, vt = load_kv(cols)
            tk = kt.shape[0]
            tri = tri_ref[:tk, :tk]
            q2 = q_ref[0, :, cols]
            zero = jnp.zeros_like(q2)
            qq = jnp.concatenate([jnp.where(low, q2, zero), jnp.where(low, zero, q2)], axis=0)
            z = lax.dot_general(qq, kt, (((1,), (1,)), ((), ())),
                                preferred_element_type=f32)
            sp = jnp.maximum(z, 0.0) + jnp.log2(1.0 + jnp.exp2(-jnp.abs(z)))
            if valid is not None:
                sp = jnp.where(valid, sp, 0.0)
            spb = sp.astype(bf16)
            c = c_ref[p]
            after = jnp.tile(c, (1, tk // LANES)) + jnp.dot(spb, tri, preferred_element_type=f32)
            a = jnp.exp2((z - sp) - after)
            if valid is not None:
                a = jnp.where(valid, a, 0.0)
            pv = jnp.dot(a.astype(bf16), vt, preferred_element_type=f32)
            c_new = after[:, 0:1] + spb[:, 0:1].astype(f32)
            c_ref[p] = jnp.broadcast_to(c_new, c.shape)
            acc_ref[:, cols] += jnp.where(low, pv[:tq], pv[tq:])

    def seq_tile(j):
        off = pl.multiple_of(j * tq, tq)
        return lambda cols: (k_ref[0, pl.ds(off, tq), cols], v_ref[0, pl.ds(off, tq), cols])

    def more_needed():
        return jnp.min(c_ref[...]) < STAY_SUM_EXIT

    acc_ref[...] = jnp.zeros_like(acc_ref)
    c_ref[...] = jnp.zeros_like(c_ref)

    row = lax.broadcasted_iota(jnp.int32, (2 * tq, tq), 0)
    col = lax.broadcasted_iota(jnp.int32, (2 * tq, tq), 1)
    visit(seq_tile(i), col < (row & (tq - 1)))

    @pl.when(i > 0)
    def _():
        visit(seq_tile(i - 1), None)

    def cond(carry):
        j, more = carry
        return jnp.logical_and(j >= 0, more)

    def body(carry):
        j, _ = carry
        visit(seq_tile(j), None)
        return j - 1, more_needed()

    _, more = lax.while_loop(cond, body, (i - 2, more_needed()))

    @pl.when(more)
    def _():
        mcol = lax.broadcasted_iota(jnp.int32, (2 * tq, LANES), 1)
        visit(lambda cols: (km_ref[:, cols], vm_ref[:, cols]), mcol < N_META)

    o_ref[0] = acc_ref[...].astype(o_ref.dtype)


def _attention(q, k, v, km, vm, tq):
    b, s, w = q.shape
    n_pairs = w // HEAD_PAIR
    r = lax.broadcasted_iota(jnp.int32, (tq, tq), 0)
    c = lax.broadcasted_iota(jnp.int32, (tq, tq), 1)
    tri = (r > c).astype(bf16)
    seq = pl.BlockSpec((1, s, w), lambda bi, i: (bi, 0, 0), pipeline_mode=pl.Buffered(1))
    tile = pl.BlockSpec((1, tq, w), lambda bi, i: (bi, i, 0))
    return pl.pallas_call(
        functools.partial(_attn_kernel, tq=tq, n_pairs=n_pairs),
        grid=(b, s // tq),
        in_specs=[tile, seq, seq, _const_spec((LANES, w)), _const_spec((LANES, w)),
                  _const_spec((tq, tq))],
        out_specs=tile,
        out_shape=jax.ShapeDtypeStruct((b, s, w), bf16),
        scratch_shapes=[pltpu.VMEM((tq, w), f32), pltpu.VMEM((n_pairs, 2 * tq, LANES), f32)],
        compiler_params=pltpu.CompilerParams(
            dimension_semantics=("arbitrary", "arbitrary"), vmem_limit_bytes=VMEM_LIMIT),
        name="sb_attention",
    )(q, k, v, km, vm, tri)


def _mix_kernel(att_ref, u_ref, prev_ref, mhalo_ref, gs_ref, gc_ref, x_ref,
                wsb_ref, wcv_ref, wout_ref, cw_ref, cb_ref, lng_ref, lnb_ref, nf_ref,
                wrt_ref, brt_ref, upper_ref,
                h_ref, xn_ref, route_ref, wtok_ref, cnt_out_ref,
                ubuf_ref, shift_ref, ycv_ref, cnt_ref, *, tm, tiles_per_seq, chunk):
    i = pl.program_id(0)
    first = (i % tiles_per_seq) == 0
    halo = jnp.where(first, mhalo_ref[...], prev_ref[...])
    ubuf_ref[0:CONV_HALO, :] = halo.astype(f32)
    ubuf_ref[CONV_HALO:, :] = u_ref[...].astype(f32)

    lead = CONV_HALO - (CONV_KERNEL - 1)
    span = shift_ref.shape[1]
    for ph in range(1, SUBLANES):
        shift_ref[ph - 1] = ubuf_ref[ph:ph + span, :]

    def tap(r0, kk):
        ph = (lead + kk) % SUBLANES
        base = r0 + lead + kk - ph
        if ph == 0:
            return ubuf_ref[base:base + chunk, :]
        return shift_ref[ph - 1, base:base + chunk, :]

    for r0 in range(0, tm, chunk):
        y = jnp.broadcast_to(cb_ref[...], (chunk, CONV_WIDTH))
        for kk in range(CONV_KERNEL):
            y = y + cw_ref[kk:kk + 1, :] * tap(r0, kk)
        mu = jnp.mean(y, axis=-1, keepdims=True)
        yc = y - mu
        var = jnp.mean(yc * yc, axis=-1, keepdims=True)
        yn = yc * lax.rsqrt(var + EPS) * lng_ref[...] + lnb_ref[...]
        ycv_ref[r0:r0 + chunk, :] = (yn * _sigmoid(yn)).astype(bf16)

    y_sb = jnp.dot(att_ref[...], wsb_ref[...], preferred_element_type=f32)
    y_cv = jnp.dot(ycv_ref[...], wcv_ref[...], preferred_element_type=f32)
    merged = gs_ref[...].astype(f32) * y_sb + gc_ref[...].astype(f32) * y_cv
    h = x_ref[...] + jnp.dot(merged.astype(bf16), wout_ref[...], preferred_element_type=f32)
    h_ref[...] = h
    xn = _rms(h, nf_ref[...])
    xnb = xn.astype(bf16)
    xn_ref[...] = xn

    lt = lax.dot_general(wrt_ref[...], xnb, (((1,), (1,)), ((), ())),
                         preferred_element_type=f32) + brt_ref[...]
    gl = [lt[N_EXPERTS + g:N_EXPERTS + g + 1, :] for g in range(N_GROUPS)]
    gmax = functools.reduce(jnp.maximum, gl)
    gsum = functools.reduce(lambda a, b: a + b, [jnp.exp(x - gmax) for x in gl])
    g_p = 1.0 / gsum
    gidx = jnp.full_like(gmax, N_GROUPS - 1)
    for g in range(N_GROUPS - 2, -1, -1):
        gidx = jnp.where(gl[g] == gmax, float(g), gidx)
    el = []
    for j in range(EXPERTS_PER_GROUP):
        x = lt[(N_GROUPS - 1) * EXPERTS_PER_GROUP + j:(N_GROUPS - 1) * EXPERTS_PER_GROUP + j + 1, :]
        for g in range(N_GROUPS - 2, -1, -1):
            x = jnp.where(gidx == float(g), lt[g * EXPERTS_PER_GROUP + j:g * EXPERTS_PER_GROUP + j + 1, :], x)
        el.append(x)
    m1 = functools.reduce(jnp.maximum, el)
    i1 = jnp.full_like(m1, EXPERTS_PER_GROUP - 1)
    for j in range(EXPERTS_PER_GROUP - 2, -1, -1):
        i1 = jnp.where(el[j] == m1, float(j), i1)
    el2 = [jnp.where(i1 == float(j), -jnp.inf, el[j]) for j in range(EXPERTS_PER_GROUP)]
    m2 = functools.reduce(jnp.maximum, el2)
    i2 = jnp.full_like(m2, EXPERTS_PER_GROUP - 1)
    for j in range(EXPERTS_PER_GROUP - 2, -1, -1):
        i2 = jnp.where(el2[j] == m2, float(j), i2)
    w1 = g_p / (1.0 + jnp.exp(m2 - m1))
    w2 = g_p - w1
    e1 = gidx * EXPERTS_PER_GROUP + i1
    e2 = gidx * EXPERTS_PER_GROUP + i2

    @pl.when(i == 0)
    def _():
        cnt_ref[...] = jnp.zeros_like(cnt_ref)

    eid = lax.broadcasted_iota(jnp.int32, (N_EXPERTS, tm), 0).astype(f32)
    is1 = eid == e1
    is2 = eid == e2
    member = jnp.where(jnp.logical_or(is1, is2), 1.0, 0.0).astype(bf16)
    before = jnp.dot(member, upper_ref[...], preferred_element_type=f32)
    cnt = cnt_ref[...]
    rank = before + jnp.tile(cnt, (1, tm // LANES))
    cnt_ref[...] = cnt + jnp.dot(member, jnp.ones((tm, LANES), bf16), preferred_element_type=f32)
    cnt_out_ref[...] = cnt_ref[...]
    r1 = jnp.sum(jnp.where(is1, rank, 0.0), axis=0, keepdims=True)
    r2 = jnp.sum(jnp.where(is2, rank, 0.0), axis=0, keepdims=True)
    pad = jnp.zeros((4, tm), f32)
    route_ref[...] = jnp.concatenate([e1, e2, r1, r2, pad], axis=0).astype(jnp.int32)
    wrow = lax.broadcasted_iota(jnp.int32, (LANES, tm), 0)
    w_t = jnp.where(wrow == 0, w1, jnp.where(wrow == 1, w2, 0.0))
    wtok_ref[...] = w_t.T


def _mix(att, u, mhalo, gs, gc, x2, wsb, wcv, wout, cw, cb, lng, lnb, nf, wrt, brt, tm, seq_len):
    t, d = x2.shape
    tiles_per_seq = seq_len // tm
    row = lambda w: pl.BlockSpec((tm, w), lambda i: (i, 0))
    halo_blocks = tm // CONV_HALO
    prev = pl.BlockSpec((CONV_HALO, CONV_WIDTH), lambda i: (jnp.maximum(i * halo_blocks - 1, 0), 0))
    kern = functools.partial(_mix_kernel, tm=tm, tiles_per_seq=tiles_per_seq, chunk=64)
    r = lax.broadcasted_iota(jnp.int32, (tm, tm), 0)
    c = lax.broadcasted_iota(jnp.int32, (tm, tm), 1)
    upper = (r < c).astype(bf16)
    return pl.pallas_call(
        kern,
        grid=(t // tm,),
        in_specs=[row(SB_WIDTH), row(CONV_WIDTH), prev, _const_spec((CONV_HALO, CONV_WIDTH)),
                  row(d), row(d), row(d),
                  _const_spec((SB_WIDTH, d)), _const_spec((CONV_WIDTH, d)), _const_spec((d, d)),
                  _const_spec((CONV_KERNEL, CONV_WIDTH)), _const_spec((1, CONV_WIDTH)),
                  _const_spec((1, CONV_WIDTH)), _const_spec((1, CONV_WIDTH)), _const_spec((1, d)),
                  _const_spec((ROUTER_ROWS, d)), _const_spec((ROUTER_ROWS, 1)), _const_spec((tm, tm))],
        out_specs=[row(d), row(d), pl.BlockSpec((8, tm), lambda i: (0, i)), row(LANES),
                   _const_spec((N_EXPERTS, LANES))],
        out_shape=[jax.ShapeDtypeStruct((t, d), f32), jax.ShapeDtypeStruct((t, d), f32),
                   jax.ShapeDtypeStruct((8, t), jnp.int32), jax.ShapeDtypeStruct((t, LANES), f32),
                   jax.ShapeDtypeStruct((N_EXPERTS, LANES), f32)],
        scratch_shapes=[pltpu.VMEM((tm + CONV_HALO, CONV_WIDTH), f32),
                        pltpu.VMEM((SUBLANES - 1, tm + CONV_HALO - SUBLANES, CONV_WIDTH), f32),
                        pltpu.VMEM((tm, CONV_WIDTH), bf16),
                        pltpu.VMEM((N_EXPERTS, LANES), f32)],
        compiler_params=pltpu.CompilerParams(
            dimension_semantics=("arbitrary",), vmem_limit_bytes=VMEM_LIMIT),
        name="mix",
    )(att, u, u, mhalo, gs, gc, x2, wsb, wcv, wout, cw, cb, lng, lnb, nf, wrt, brt, upper)


def _row_copy(src, src_row, dst, dst_row, sem):
    return pltpu.make_async_copy(src.at[pl.ds(src_row, 1)], dst.at[pl.ds(dst_row, 1)], sem)


def _scatter_kernel(pos_ref, x_ref, zeros_hbm, xs_hbm, xbuf, sems, *, tm):
    del zeros_hbm
    i = pl.program_id(0)
    slot = i % 2
    xbuf[slot] = x_ref[...]
    src = xbuf.at[slot]
    for r in range(tm):
        for c in range(2):
            _row_copy(src, r, xs_hbm, pos_ref[0, c, r], sems.at[slot]).start(priority=c)

    def wait_tile(sl):
        for _ in range(2):
            pltpu.make_async_copy(xbuf.at[sl], xs_hbm.at[pl.ds(0, tm)], sems.at[sl]).wait()

    @pl.when(i > 0)
    def _():
        wait_tile(1 - slot)

    @pl.when(i == pl.num_programs(0) - 1)
    def _():
        wait_tile(slot)


def _scatter(pos3, xpk, n_rows):
    n_tiles, _, tm = pos3.shape
    w = xpk.shape[1]
    any_spec = pl.BlockSpec(memory_space=pl.ANY)
    return pl.pallas_call(
        functools.partial(_scatter_kernel, tm=tm),
        grid=(n_tiles,),
        in_specs=[pl.BlockSpec((1, 2, tm), lambda i: (i, 0, 0), memory_space=pltpu.SMEM),
                  pl.BlockSpec((tm, w), lambda i: (i, 0)), any_spec],
        out_specs=any_spec,
        out_shape=jax.ShapeDtypeStruct((n_rows, w), xpk.dtype),
        scratch_shapes=[pltpu.VMEM((2, tm, w), xpk.dtype), pltpu.SemaphoreType.DMA((2,))],
        input_output_aliases={2: 0},
        compiler_params=pltpu.CompilerParams(dimension_semantics=("arbitrary",)),
        name="moe_scatter",
    )(pos3, xpk, jnp.zeros((n_rows, w), xpk.dtype))


def _experts_kernel(te_ref, nu_ref, xs_ref, wgu_ref, wd_ref, ys_ref):
    del te_ref
    i = pl.program_id(0)

    @pl.when(i < nu_ref[0])
    def _():
        gu = jnp.dot(xs_ref[...].astype(bf16), wgu_ref[0], preferred_element_type=f32)
        g = gu[:, :EXPERT_FF]
        hdn = (g * _sigmoid(g)) * gu[:, EXPERT_FF:]
        ys_ref[...] = jnp.dot(hdn.astype(bf16), wd_ref[0], preferred_element_type=f32)

    @pl.when(i >= nu_ref[0])
    def _():
        ys_ref[...] = jnp.zeros_like(ys_ref)


def _experts(tile_expert, n_used, xs, wgu, wd, tm):
    n_rows, w = xs.shape
    d = wd.shape[2]
    grid_spec = pltpu.PrefetchScalarGridSpec(
        num_scalar_prefetch=2,
        grid=(n_rows // tm,),
        in_specs=[pl.BlockSpec((tm, w), lambda i, te, nu: (i, 0)),
                  pl.BlockSpec((1, d, 2 * EXPERT_FF), lambda i, te, nu: (te[i], 0, 0)),
                  pl.BlockSpec((1, EXPERT_FF, d), lambda i, te, nu: (te[i], 0, 0))],
        out_specs=pl.BlockSpec((tm, w), lambda i, te, nu: (i, 0)),
    )
    return pl.pallas_call(
        _experts_kernel,
        grid_spec=grid_spec,
        out_shape=jax.ShapeDtypeStruct((n_rows, w), f32),
        compiler_params=pltpu.CompilerParams(
            dimension_semantics=("arbitrary",), vmem_limit_bytes=VMEM_LIMIT),
        name="moe_experts",
    )(tile_expert, n_used, xs, wgu, wd)


def _combine_kernel(pos_ref, nxt_ref, ys_hbm, wtok_ref, h_ref, nf_ref, o_ref, gbuf, sems, *, tm):
    i = pl.program_id(0)
    slot = i % 2

    def issue(p_ref, sl):
        for r in range(tm):
            for c in range(2):
                _row_copy(ys_hbm, p_ref[0, c, r], gbuf.at[sl, c], r, sems.at[sl]).start(priority=c)

    @pl.when(i == 0)
    def _():
        issue(pos_ref, 0)

    @pl.when(i + 1 < pl.num_programs(0))
    def _():
        issue(nxt_ref, 1 - slot)

    for c in range(2):
        pltpu.make_async_copy(ys_hbm.at[pl.ds(0, tm)], gbuf.at[slot, c], sems.at[slot]).wait()

    wt = wtok_ref[...]
    y = gbuf[slot, 0] * wt[:, 0:1] + gbuf[slot, 1] * wt[:, 1:2]
    o_ref[...] = _rms(h_ref[...] + y, nf_ref[...])


def _combine(pos3, ys, wtok, h, nf):
    n_tiles, _, tm = pos3.shape
    t, d = h.shape
    w = ys.shape[1]
    row = lambda wd_: pl.BlockSpec((tm, wd_), lambda i: (i, 0))
    last = n_tiles - 1
    return pl.pallas_call(
        functools.partial(_combine_kernel, tm=tm),
        grid=(n_tiles,),
        in_specs=[pl.BlockSpec((1, 2, tm), lambda i: (i, 0, 0), memory_space=pltpu.SMEM),
                  pl.BlockSpec((1, 2, tm), lambda i: (jnp.minimum(i + 1, last), 0, 0),
                               memory_space=pltpu.SMEM),
                  pl.BlockSpec(memory_space=pl.ANY), row(LANES), row(d), _const_spec((1, d))],
        out_specs=row(d),
        out_shape=jax.ShapeDtypeStruct((t, d), f32),
        scratch_shapes=[pltpu.VMEM((2, 2, tm, w), f32), pltpu.SemaphoreType.DMA((2,))],
        compiler_params=pltpu.CompilerParams(
            dimension_semantics=("arbitrary",), vmem_limit_bytes=VMEM_LIMIT),
        name="moe_combine",
    )(pos3, pos3, ys, wtok, h, nf)


def _tile_positions(pos, tm):
    return pos.reshape(2, -1, tm).transpose(1, 0, 2)


def kernel(x, meta, norm_mix, w_in, w_sb_o, conv_w, conv_b, conv_ln_g, conv_ln_b, w_conv_o, w_out,
           norm_ffn, w_router_group, b_router_group, w_router_expert, b_router_expert, w_gate, w_up,
           w_down, norm_final):
    assert norm_mix.shape[0] == 1, "single-layer block"
    b, s, d = x.shape
    t = b * s
    tm = min(512, s)
    tq = min(256, s)
    x2 = x.reshape(t, d)
    w_in_b = w_in[0].astype(bf16)
    g_mix = norm_mix[0][None]

    q, k, v, u, gs, gc = _in_proj(x2, g_mix, w_in_b, tm)
    _, km, vm, um, _, _ = _in_proj(meta, g_mix, w_in_b, N_META)
    pad_rows = lambda a, top, bottom: jnp.pad(a, ((top, bottom), (0, 0)))
    km = pad_rows(km, 0, LANES - N_META)
    vm = pad_rows(vm, 0, LANES - N_META)
    shp = (b, s, SB_WIDTH)
    att = _attention(q.reshape(shp), k.reshape(shp), v.reshape(shp), km, vm, tq).reshape(t, SB_WIDTH)

    mhalo = pad_rows(um, CONV_HALO - N_META, 0)
    wrt = jnp.concatenate([w_router_expert[0], w_router_group[0]], axis=1).T
    wrt = pad_rows(wrt, 0, ROUTER_ROWS - wrt.shape[0]).astype(bf16)
    brt = jnp.concatenate([b_router_expert[0], b_router_group[0]])[:, None]
    brt = pad_rows(brt, 0, ROUTER_ROWS - brt.shape[0]).astype(f32)
    h, xpk, route, wtok, counts = _mix(
        att, u, mhalo, gs, gc, x2,
        w_sb_o[0].astype(bf16), w_conv_o[0].astype(bf16), w_out[0].astype(bf16),
        conv_w[0], conv_b[0][None], conv_ln_g[0][None], conv_ln_b[0][None],
        norm_ffn[0][None], wrt, brt, tm, s)

    te = min(512, s)
    n_rows = 2 * t + N_EXPERTS * te
    cnt = counts[:, 0].astype(jnp.int32)
    tiles = (cnt + te - 1) // te
    tile_end = jnp.cumsum(tiles)
    row_start = (tile_end - tiles) * te
    experts = jnp.arange(N_EXPERTS, dtype=jnp.int32)
    onehot = route[0:2, :, None] == experts
    pos = jnp.sum(jnp.where(onehot, row_start, 0), axis=-1) + route[2:4]
    tile_ids = jnp.arange(n_rows // te, dtype=jnp.int32)
    tile_expert = jnp.minimum(jnp.sum(tile_ids[:, None] >= tile_end, axis=-1), N_EXPERTS - 1)
    tile_expert = tile_expert.astype(jnp.int32)
    n_used = tile_end[-1:].astype(jnp.int32)

    xs = _scatter(_tile_positions(pos, min(512, s)), xpk, n_rows)
    wgu = jnp.concatenate([w_gate[0], w_up[0]], axis=-1).astype(bf16)
    ys = _experts(tile_expert, n_used, xs, wgu, w_down[0].astype(bf16), te)
    out = _combine(_tile_positions(pos, min(256, s)), ys, wtok, h, norm_final[None])
    return out.reshape(b, s, d)
```

```python
import functools

import jax
import jax.numpy as jnp
from jax import lax
from jax.experimental import pallas as pl
from jax.experimental.pallas import tpu as pltpu

N_META = 16
SB_HEAD_DIM = 64
SB_WIDTH = 512
CONV_WIDTH = 512
CONV_KERNEL = 31
N_GROUPS = 4
EXPERTS_PER_GROUP = 4
N_EXPERTS = N_GROUPS * EXPERTS_PER_GROUP
EXPERT_FF = 512
GROUP_PAIRS = [(a, b) for a in range(EXPERTS_PER_GROUP) for b in range(a + 1, EXPERTS_PER_GROUP)]
PAIRS_PER_GROUP = len(GROUP_PAIRS)
N_CLASSES = N_GROUPS * PAIRS_PER_GROUP
CLASS_ROWS = 32
EPS = 1e-6

LANES = 128
SUBLANES = 8
HEAD_PAIR = 2 * SB_HEAD_DIM
CONV_HALO = 32
ROUTER_ROWS = 32
STAY_SUM_EXIT = 150.0
LOG2_E = 1.4426950408889634
VMEM_LIMIT = 56 * 1024 * 1024

f32 = jnp.float32
bf16 = jnp.bfloat16


def _sigmoid(x):
    return 1.0 / (1.0 + jnp.exp(-x))


def _rms(x, g):
    return x * lax.rsqrt(jnp.mean(x * x, axis=-1, keepdims=True) + EPS) * g


def _const_spec(shape):
    return pl.BlockSpec(shape, lambda *_: (0,) * len(shape))


def _in_proj_kernel(x_ref, g_ref, w_ref, q_ref, k_ref, v_ref, u_ref, gs_ref, gc_ref):
    xb = _rms(x_ref[...], g_ref[...]).astype(bf16)

    def proj(lo, width):
        return jnp.dot(xb, w_ref[:, lo:lo + width], preferred_element_type=f32)

    o = 0
    q_ref[...] = (proj(o, SB_WIDTH) * (SB_HEAD_DIM ** -0.5 * LOG2_E)).astype(bf16); o += SB_WIDTH
    k_ref[...] = proj(o, SB_WIDTH).astype(bf16); o += SB_WIDTH
    v_ref[...] = proj(o, SB_WIDTH).astype(bf16); o += SB_WIDTH
    a = proj(o, CONV_WIDTH); o += CONV_WIDTH
    b = proj(o, CONV_WIDTH); o += CONV_WIDTH
    u_ref[...] = (a * _sigmoid(b)).astype(bf16)
    d = gs_ref.shape[1]
    gs_ref[...] = _sigmoid(proj(o, d)).astype(bf16); o += d
    gc_ref[...] = _sigmoid(proj(o, d)).astype(bf16)


def _in_proj(x2, g, w_b, tm):
    t, d = x2.shape
    n_in = w_b.shape[1]
    row = lambda w: pl.BlockSpec((tm, w), lambda i: (i, 0))
    sds = lambda w: jax.ShapeDtypeStruct((t, w), bf16)
    return pl.pallas_call(
        _in_proj_kernel,
        grid=(t // tm,),
        in_specs=[row(d), _const_spec((1, d)), _const_spec((d, n_in))],
        out_specs=[row(SB_WIDTH), row(SB_WIDTH), row(SB_WIDTH), row(CONV_WIDTH), row(d), row(d)],
        out_shape=[sds(SB_WIDTH), sds(SB_WIDTH), sds(SB_WIDTH), sds(CONV_WIDTH), sds(d), sds(d)],
        compiler_params=pltpu.CompilerParams(
            dimension_semantics=("arbitrary",), vmem_limit_bytes=VMEM_LIMIT),
        name="in_proj",
    )(x2, g, w_b)


def _attn_kernel(q_ref, k_ref, v_ref, km_ref, vm_ref, tri_ref, o_ref, acc_ref, c_ref, *, tq, n_pairs):
    i = pl.program_id(1)
    lane = lax.broadcasted_iota(jnp.int32, (1, LANES), 1)
    low = lane < SB_HEAD_DIM

    def visit(load_kv, valid):
        for p in range(n_pairs):
            cols = slice(p * HEAD_PAIR, (p + 1) * HEAD_PAIR)
            kt, vt = load_kv(cols)
            tk = kt.shape[0]
            tri = tri_ref[:tk, :tk]
            q2 = q_ref[0, :, cols]
            zero = jnp.zeros_like(q2)
            qq = jnp.concatenate([jnp.where(low, q2, zero), jnp.where(low, zero, q2)], axis=0)
            z = lax.dot_general(qq, kt, (((1,), (1,)), ((), ())),
                                preferred_element_type=f32)
            sp = jnp.maximum(z, 0.0) + jnp.log2(1.0 + jnp.exp2(-jnp.abs(z)))
            if valid is not None:
                sp = jnp.where(valid, sp, 0.0)
            spb = sp.astype(bf16)
            c = c_ref[p]
            after = jnp.tile(c, (1, tk // LANES)) + jnp.dot(spb, tri, preferred_element_type=f32)
            a = jnp.exp2((z - sp) - after)
            if valid is not None:
                a = jnp.where(valid, a, 0.0)
            pv = jnp.dot(a.astype(bf16), vt, preferred_element_type=f32)
            c_new = after[:, 0:1] + spb[:, 0:1].astype(f32)
            c_ref[p] = jnp.broadcast_to(c_new, c.shape)
            acc_ref[:, cols] += jnp.where(low, pv[:tq], pv[tq:])

    def seq_tile(j):
        off = pl.multiple_of(j * tq, tq)
        return lambda cols: (k_ref[0, pl.ds(off, tq), cols], v_ref[0, pl.ds(off, tq), cols])

    def more_needed():
        return jnp.min(c_ref[...]) < STAY_SUM_EXIT

    acc_ref[...] = jnp.zeros_like(acc_ref)
    c_ref[...] = jnp.zeros_like(c_ref)

    row = lax.broadcasted_iota(jnp.int32, (2 * tq, tq), 0)
    col = lax.broadcasted_iota(jnp.int32, (2 * tq, tq), 1)
    visit(seq_tile(i), col < (row & (tq - 1)))

    @pl.when(i > 0)
    def _():
        visit(seq_tile(i - 1), None)

    def cond(carry):
        j, more = carry
        return jnp.logical_and(j >= 0, more)

    def body(carry):
        j, _ = carry
        visit(seq_tile(j), None)
        return j - 1, more_needed()

    _, more = lax.while_loop(cond, body, (i - 2, more_needed()))

    @pl.when(more)
    def _():
        mcol = lax.broadcasted_iota(jnp.int32, (2 * tq, LANES), 1)
        visit(lambda cols: (km_ref[:, cols], vm_ref[:, cols]), mcol < N_META)

    o_ref[0] = acc_ref[...].astype(o_ref.dtype)


def _attention(q, k, v, km, vm, tq):
    b, s, w = q.shape
    n_pairs = w // HEAD_PAIR
    r = lax.broadcasted_iota(jnp.int32, (tq, tq), 0)
    c = lax.broadcasted_iota(jnp.int32, (tq, tq), 1)
    tri = (r > c).astype(bf16)
    seq = pl.BlockSpec((1, s, w), lambda bi, i: (bi, 0, 0), pipeline_mode=pl.Buffered(1))
    tile = pl.BlockSpec((1, tq, w), lambda bi, i: (bi, i, 0))
    return pl.pallas_call(
        functools.partial(_attn_kernel, tq=tq, n_pairs=n_pairs),
        grid=(b, s // tq),
        in_specs=[tile, seq, seq, _const_spec((LANES, w)), _const_spec((LANES, w)),
                  _const_spec((tq, tq))],
        out_specs=tile,
        out_shape=jax.ShapeDtypeStruct((b, s, w), bf16),
        scratch_shapes=[pltpu.VMEM((tq, w), f32), pltpu.VMEM((n_pairs, 2 * tq, LANES), f32)],
        compiler_params=pltpu.CompilerParams(
            dimension_semantics=("arbitrary", "arbitrary"), vmem_limit_bytes=VMEM_LIMIT),
        name="sb_attention",
    )(q, k, v, km, vm, tri)


def _mix_kernel(att_ref, u_ref, prev_ref, mhalo_ref, gs_ref, gc_ref, x_ref,
                wsb_ref, wcv_ref, wout_ref, cw_ref, cb_ref, lng_ref, lnb_ref, nf_ref,
                wrt_ref, brt_ref, upper_ref,
                h_ref, xn_ref, route_ref, cnt_out_ref,
                ubuf_ref, shift_ref, ycv_ref, cnt_ref, *, tm, tiles_per_seq, chunk):
    i = pl.program_id(0)
    first = (i % tiles_per_seq) == 0
    halo = jnp.where(first, mhalo_ref[...], prev_ref[...])
    ubuf_ref[0:CONV_HALO, :] = halo.astype(f32)
    ubuf_ref[CONV_HALO:, :] = u_ref[...].astype(f32)

    lead = CONV_HALO - (CONV_KERNEL - 1)
    span = shift_ref.shape[1]
    for ph in range(1, SUBLANES):
        shift_ref[ph - 1] = ubuf_ref[ph:ph + span, :]

    def tap(r0, kk):
        ph = (lead + kk) % SUBLANES
        base = r0 + lead + kk - ph
        if ph == 0:
            return ubuf_ref[base:base + chunk, :]
        return shift_ref[ph - 1, base:base + chunk, :]

    for r0 in range(0, tm, chunk):
        y = jnp.broadcast_to(cb_ref[...], (chunk, CONV_WIDTH))
        for kk in range(CONV_KERNEL):
            y = y + cw_ref[kk:kk + 1, :] * tap(r0, kk)
        mu = jnp.mean(y, axis=-1, keepdims=True)
        yc = y - mu
        var = jnp.mean(yc * yc, axis=-1, keepdims=True)
        yn = yc * lax.rsqrt(var + EPS) * lng_ref[...] + lnb_ref[...]
        ycv_ref[r0:r0 + chunk, :] = (yn * _sigmoid(yn)).astype(bf16)

    y_sb = jnp.dot(att_ref[...], wsb_ref[...], preferred_element_type=f32)
    y_cv = jnp.dot(ycv_ref[...], wcv_ref[...], preferred_element_type=f32)
    merged = gs_ref[...].astype(f32) * y_sb + gc_ref[...].astype(f32) * y_cv
    h = x_ref[...] + jnp.dot(merged.astype(bf16), wout_ref[...], preferred_element_type=f32)
    h_ref[...] = h
    xn = _rms(h, nf_ref[...])
    xnb = xn.astype(bf16)
    d_model = xn.shape[1]
    xn_ref[:, :d_model] = xn

    lt = lax.dot_general(wrt_ref[...], xnb, (((1,), (1,)), ((), ())),
                         preferred_element_type=f32) + brt_ref[...]
    gl = [lt[N_EXPERTS + g:N_EXPERTS + g + 1, :] for g in range(N_GROUPS)]
    gmax = functools.reduce(jnp.maximum, gl)
    gsum = functools.reduce(lambda a, b: a + b, [jnp.exp(x - gmax) for x in gl])
    g_p = 1.0 / gsum
    gidx = jnp.full_like(gmax, N_GROUPS - 1)
    for g in range(N_GROUPS - 2, -1, -1):
        gidx = jnp.where(gl[g] == gmax, float(g), gidx)
    el = []
    for j in range(EXPERTS_PER_GROUP):
        x = lt[(N_GROUPS - 1) * EXPERTS_PER_GROUP + j:(N_GROUPS - 1) * EXPERTS_PER_GROUP + j + 1, :]
        for g in range(N_GROUPS - 2, -1, -1):
            x = jnp.where(gidx == float(g), lt[g * EXPERTS_PER_GROUP + j:g * EXPERTS_PER_GROUP + j + 1, :], x)
        el.append(x)
    m1 = functools.reduce(jnp.maximum, el)
    i1 = jnp.full_like(m1, EXPERTS_PER_GROUP - 1)
    for j in range(EXPERTS_PER_GROUP - 2, -1, -1):
        i1 = jnp.where(el[j] == m1, float(j), i1)
    el2 = [jnp.where(i1 == float(j), -jnp.inf, el[j]) for j in range(EXPERTS_PER_GROUP)]
    m2 = functools.reduce(jnp.maximum, el2)
    i2 = jnp.full_like(m2, EXPERTS_PER_GROUP - 1)
    for j in range(EXPERTS_PER_GROUP - 2, -1, -1):
        i2 = jnp.where(el2[j] == m2, float(j), i2)
    w1 = g_p / (1.0 + jnp.exp(m2 - m1))
    w2 = g_p - w1
    lo = jnp.minimum(i1, i2)
    hi = jnp.maximum(i1, i2)
    pair = jnp.where(lo == 0.0, hi - 1.0, jnp.where(lo == 1.0, hi + 1.0, float(PAIRS_PER_GROUP - 1)))
    cls = gidx * PAIRS_PER_GROUP + pair
    first_is_lo = i1 < i2
    w_lo = jnp.where(first_is_lo, w1, w2)
    w_hi = jnp.where(first_is_lo, w2, w1)

    @pl.when(i == 0)
    def _():
        cnt_ref[...] = jnp.zeros_like(cnt_ref)

    in_cls = lax.broadcasted_iota(jnp.int32, (CLASS_ROWS, tm), 0).astype(f32) == cls
    member = jnp.where(in_cls, 1.0, 0.0).astype(bf16)
    before = jnp.dot(member, upper_ref[...], preferred_element_type=f32)
    cnt = cnt_ref[...]
    rank = jnp.sum(jnp.where(in_cls, before + jnp.tile(cnt, (1, tm // LANES)), 0.0),
                   axis=0, keepdims=True)
    cnt_ref[...] = cnt + jnp.dot(member, jnp.ones((tm, LANES), bf16), preferred_element_type=f32)
    cnt_out_ref[...] = cnt_ref[...]
    pad = jnp.zeros((6, tm), f32)
    route_ref[...] = jnp.concatenate([cls, rank, pad], axis=0).astype(jnp.int32)
    wrow = lax.broadcasted_iota(jnp.int32, (LANES, tm), 0)
    w_t = jnp.where(wrow == 0, w_lo, jnp.where(wrow == 1, w_hi, 0.0))
    xn_ref[:, d_model:] = w_t.T


def _mix(att, u, mhalo, gs, gc, x2, wsb, wcv, wout, cw, cb, lng, lnb, nf, wrt, brt, tm, seq_len):
    t, d = x2.shape
    tiles_per_seq = seq_len // tm
    row = lambda w: pl.BlockSpec((tm, w), lambda i: (i, 0))
    halo_blocks = tm // CONV_HALO
    prev = pl.BlockSpec((CONV_HALO, CONV_WIDTH), lambda i: (jnp.maximum(i * halo_blocks - 1, 0), 0))
    kern = functools.partial(_mix_kernel, tm=tm, tiles_per_seq=tiles_per_seq, chunk=64)
    r = lax.broadcasted_iota(jnp.int32, (tm, tm), 0)
    c = lax.broadcasted_iota(jnp.int32, (tm, tm), 1)
    upper = (r < c).astype(bf16)
    return pl.pallas_call(
        kern,
        grid=(t // tm,),
        in_specs=[row(SB_WIDTH), row(CONV_WIDTH), prev, _const_spec((CONV_HALO, CONV_WIDTH)),
                  row(d), row(d), row(d),
                  _const_spec((SB_WIDTH, d)), _const_spec((CONV_WIDTH, d)), _const_spec((d, d)),
                  _const_spec((CONV_KERNEL, CONV_WIDTH)), _const_spec((1, CONV_WIDTH)),
                  _const_spec((1, CONV_WIDTH)), _const_spec((1, CONV_WIDTH)), _const_spec((1, d)),
                  _const_spec((ROUTER_ROWS, d)), _const_spec((ROUTER_ROWS, 1)), _const_spec((tm, tm))],
        out_specs=[row(d), row(d + LANES), pl.BlockSpec((8, tm), lambda i: (0, i)),
                   _const_spec((CLASS_ROWS, LANES))],
        out_shape=[jax.ShapeDtypeStruct((t, d), f32), jax.ShapeDtypeStruct((t, d + LANES), f32),
                   jax.ShapeDtypeStruct((8, t), jnp.int32),
                   jax.ShapeDtypeStruct((CLASS_ROWS, LANES), f32)],
        scratch_shapes=[pltpu.VMEM((tm + CONV_HALO, CONV_WIDTH), f32),
                        pltpu.VMEM((SUBLANES - 1, tm + CONV_HALO - SUBLANES, CONV_WIDTH), f32),
                        pltpu.VMEM((tm, CONV_WIDTH), bf16),
                        pltpu.VMEM((CLASS_ROWS, LANES), f32)],
        compiler_params=pltpu.CompilerParams(
            dimension_semantics=("arbitrary",), vmem_limit_bytes=VMEM_LIMIT),
        name="mix",
    )(att, u, u, mhalo, gs, gc, x2, wsb, wcv, wout, cw, cb, lng, lnb, nf, wrt, brt, upper)


def _row_copy(src, src_row, dst, dst_row, sem):
    return pltpu.make_async_copy(src.at[pl.ds(src_row, 1)], dst.at[pl.ds(dst_row, 1)], sem)


def _scatter_kernel(pos_ref, x_ref, zeros_hbm, xs_hbm, xbuf, sems, *, tm):
    del zeros_hbm
    i = pl.program_id(0)
    slot = i % 2
    xbuf[slot] = x_ref[...]
    src = xbuf.at[slot]
    for r in range(tm):
        _row_copy(src, r, xs_hbm, pos_ref[0, 0, r], sems.at[slot]).start(priority=r % 2)

    def wait_tile(sl):
        pltpu.make_async_copy(xbuf.at[sl], xs_hbm.at[pl.ds(0, tm)], sems.at[sl]).wait()

    @pl.when(i > 0)
    def _():
        wait_tile(1 - slot)

    @pl.when(i == pl.num_programs(0) - 1)
    def _():
        wait_tile(slot)


def _scatter(pos3, rows, n_rows):
    n_tiles, _, tm = pos3.shape
    w = rows.shape[1]
    any_spec = pl.BlockSpec(memory_space=pl.ANY)
    return pl.pallas_call(
        functools.partial(_scatter_kernel, tm=tm),
        grid=(n_tiles,),
        in_specs=[pl.BlockSpec((1, 1, tm), lambda i: (i, 0, 0), memory_space=pltpu.SMEM),
                  pl.BlockSpec((tm, w), lambda i: (i, 0)), any_spec],
        out_specs=any_spec,
        out_shape=jax.ShapeDtypeStruct((n_rows, w), rows.dtype),
        scratch_shapes=[pltpu.VMEM((2, tm, w), rows.dtype), pltpu.SemaphoreType.DMA((2,))],
        input_output_aliases={2: 0},
        compiler_params=pltpu.CompilerParams(dimension_semantics=("arbitrary",)),
        name="moe_scatter",
    )(pos3, rows, jnp.zeros((n_rows, w), rows.dtype))


def _experts_kernel(lo_ref, hi_ref, nu_ref, xs_ref, wgu_lo_ref, wd_lo_ref, wgu_hi_ref, wd_hi_ref,
                    ys_ref):
    del lo_ref, hi_ref
    i = pl.program_id(0)
    d = ys_ref.shape[1]

    @pl.when(i < nu_ref[0])
    def _():
        x = xs_ref[:, :d].astype(bf16)
        wts = xs_ref[:, d:]

        def scaled_hidden(wgu_ref, col):
            gu = jnp.dot(x, wgu_ref[0], preferred_element_type=f32)
            g = gu[:, :EXPERT_FF]
            return ((g * _sigmoid(g)) * gu[:, EXPERT_FF:] * wts[:, col:col + 1]).astype(bf16)

        ys_ref[...] = (
            jnp.dot(scaled_hidden(wgu_lo_ref, 0), wd_lo_ref[0], preferred_element_type=f32)
            + jnp.dot(scaled_hidden(wgu_hi_ref, 1), wd_hi_ref[0], preferred_element_type=f32))

    @pl.when(i >= nu_ref[0])
    def _():
        ys_ref[...] = jnp.zeros_like(ys_ref)


def _experts(tile_lo, tile_hi, n_used, xs, wgu, wd, tm):
    n_rows, w = xs.shape
    d = wd.shape[2]
    up = lambda which: pl.BlockSpec((1, d, 2 * EXPERT_FF), lambda i, lo, hi, nu: ((lo, hi)[which][i], 0, 0))
    down = lambda which: pl.BlockSpec((1, EXPERT_FF, d), lambda i, lo, hi, nu: ((lo, hi)[which][i], 0, 0))
    grid_spec = pltpu.PrefetchScalarGridSpec(
        num_scalar_prefetch=3,
        grid=(n_rows // tm,),
        in_specs=[pl.BlockSpec((tm, w), lambda i, lo, hi, nu: (i, 0)), up(0), down(0), up(1), down(1)],
        out_specs=pl.BlockSpec((tm, d), lambda i, lo, hi, nu: (i, 0)),
    )
    return pl.pallas_call(
        _experts_kernel,
        grid_spec=grid_spec,
        out_shape=jax.ShapeDtypeStruct((n_rows, d), f32),
        compiler_params=pltpu.CompilerParams(
            dimension_semantics=("arbitrary",), vmem_limit_bytes=VMEM_LIMIT),
        name="moe_experts",
    )(tile_lo, tile_hi, n_used, xs, wgu, wd, wgu, wd)


def _combine_kernel(pos_ref, nxt_ref, ys_hbm, h_ref, nf_ref, o_ref, gbuf, sems, *, tm):
    i = pl.program_id(0)
    slot = i % 2

    def issue(p_ref, sl):
        for r in range(tm):
            _row_copy(ys_hbm, p_ref[0, 0, r], gbuf.at[sl], r, sems.at[sl]).start(priority=r % 2)

    @pl.when(i == 0)
    def _():
        issue(pos_ref, 0)

    @pl.when(i + 1 < pl.num_programs(0))
    def _():
        issue(nxt_ref, 1 - slot)

    pltpu.make_async_copy(ys_hbm.at[pl.ds(0, tm)], gbuf.at[slot], sems.at[slot]).wait()
    o_ref[...] = _rms(h_ref[...] + gbuf[slot], nf_ref[...])


def _combine(pos3, ys, h, nf):
    n_tiles, _, tm = pos3.shape
    t, d = h.shape
    row = pl.BlockSpec((tm, d), lambda i: (i, 0))
    last = n_tiles - 1
    return pl.pallas_call(
        functools.partial(_combine_kernel, tm=tm),
        grid=(n_tiles,),
        in_specs=[pl.BlockSpec((1, 1, tm), lambda i: (i, 0, 0), memory_space=pltpu.SMEM),
                  pl.BlockSpec((1, 1, tm), lambda i: (jnp.minimum(i + 1, last), 0, 0),
                               memory_space=pltpu.SMEM),
                  pl.BlockSpec(memory_space=pl.ANY), row, _const_spec((1, d))],
        out_specs=row,
        out_shape=jax.ShapeDtypeStruct((t, d), f32),
        scratch_shapes=[pltpu.VMEM((2, tm, d), f32), pltpu.SemaphoreType.DMA((2,))],
        compiler_params=pltpu.CompilerParams(
            dimension_semantics=("arbitrary",), vmem_limit_bytes=VMEM_LIMIT),
        name="moe_combine",
    )(pos3, pos3, ys, h, nf)


def _tile_positions(pos, tm):
    return pos.reshape(-1, 1, tm)


def kernel(x, meta, norm_mix, w_in, w_sb_o, conv_w, conv_b, conv_ln_g, conv_ln_b, w_conv_o, w_out,
           norm_ffn, w_router_group, b_router_group, w_router_expert, b_router_expert, w_gate, w_up,
           w_down, norm_final):
    assert norm_mix.shape[0] == 1, "single-layer block"
    b, s, d = x.shape
    t = b * s
    tm = min(512, s)
    tq = min(256, s)
    x2 = x.reshape(t, d)
    w_in_b = w_in[0].astype(bf16)
    g_mix = norm_mix[0][None]

    q, k, v, u, gs, gc = _in_proj(x2, g_mix, w_in_b, tm)
    _, km, vm, um, _, _ = _in_proj(meta, g_mix, w_in_b, N_META)
    pad_rows = lambda a, top, bottom: jnp.pad(a, ((top, bottom), (0, 0)))
    km = pad_rows(km, 0, LANES - N_META)
    vm = pad_rows(vm, 0, LANES - N_META)
    shp = (b, s, SB_WIDTH)
    att = _attention(q.reshape(shp), k.reshape(shp), v.reshape(shp), km, vm, tq).reshape(t, SB_WIDTH)

    mhalo = pad_rows(um, CONV_HALO - N_META, 0)
    wrt = jnp.concatenate([w_router_expert[0], w_router_group[0]], axis=1).T
    wrt = pad_rows(wrt, 0, ROUTER_ROWS - wrt.shape[0]).astype(bf16)
    brt = jnp.concatenate([b_router_expert[0], b_router_group[0]])[:, None]
    brt = pad_rows(brt, 0, ROUTER_ROWS - brt.shape[0]).astype(f32)
    h, xrow, route, counts = _mix(
        att, u, mhalo, gs, gc, x2,
        w_sb_o[0].astype(bf16), w_conv_o[0].astype(bf16), w_out[0].astype(bf16),
        conv_w[0], conv_b[0][None], conv_ln_g[0][None], conv_ln_b[0][None],
        norm_ffn[0][None], wrt, brt, tm, s)

    te = min(512, s)
    n_rows = t + N_CLASSES * te
    cnt = counts[:N_CLASSES, 0].astype(jnp.int32)
    tiles = (cnt + te - 1) // te
    tile_end = jnp.cumsum(tiles)
    row_start = (tile_end - tiles) * te
    classes = jnp.arange(N_CLASSES, dtype=jnp.int32)
    pos = jnp.sum(jnp.where(route[0][:, None] == classes, row_start, 0), axis=-1) + route[1]
    tile_ids = jnp.arange(n_rows // te, dtype=jnp.int32)
    tile_cls = jnp.minimum(jnp.sum(tile_ids[:, None] >= tile_end, axis=-1), N_CLASSES - 1)
    cls_hot = tile_cls[:, None] == classes
    lo_of = jnp.array([c // PAIRS_PER_GROUP * EXPERTS_PER_GROUP + GROUP_PAIRS[c % PAIRS_PER_GROUP][0]
                       for c in range(N_CLASSES)], jnp.int32)
    hi_of = jnp.array([c // PAIRS_PER_GROUP * EXPERTS_PER_GROUP + GROUP_PAIRS[c % PAIRS_PER_GROUP][1]
                       for c in range(N_CLASSES)], jnp.int32)
    tile_lo = jnp.sum(jnp.where(cls_hot, lo_of, 0), axis=-1).astype(jnp.int32)
    tile_hi = jnp.sum(jnp.where(cls_hot, hi_of, 0), axis=-1).astype(jnp.int32)
    n_used = tile_end[-1:].astype(jnp.int32)

    xs = _scatter(_tile_positions(pos, min(512, s)), xrow, n_rows)
    wgu = jnp.concatenate([w_gate[0], w_up[0]], axis=-1).astype(bf16)
    ys = _experts(tile_lo, tile_hi, n_used, xs, wgu, w_down[0].astype(bf16), te)
    out = _combine(_tile_positions(pos, min(256, s)), ys, h, norm_final[None])
    return out.reshape(b, s, d)
```

```python
import functools

import jax
import jax.numpy as jnp
import numpy as np
from jax import lax
from jax.experimental import pallas as pl
from jax.experimental.pallas import tpu as pltpu

N_META = 16
SB_HEAD_DIM = 64
SB_WIDTH = 512
CONV_WIDTH = 512
CONV_KERNEL = 31
N_GROUPS = 4
EXPERTS_PER_GROUP = 4
N_EXPERTS = N_GROUPS * EXPERTS_PER_GROUP
EXPERT_FF = 512
GROUP_PAIRS = [(a, b) for a in range(EXPERTS_PER_GROUP) for b in range(a + 1, EXPERTS_PER_GROUP)]
PAIRS_PER_GROUP = len(GROUP_PAIRS)
N_CLASSES = N_GROUPS * PAIRS_PER_GROUP
CLASS_ROWS = 32
EPS = 1e-6

LANES = 128
SUBLANES = 8
HEAD_PAIR = 2 * SB_HEAD_DIM
CONV_HALO = 32
ROUTER_ROWS = 32
STAY_SUM_EXIT = 150.0
LOG2_E = 1.4426950408889634
VMEM_LIMIT = 56 * 1024 * 1024

f32 = jnp.float32
bf16 = jnp.bfloat16


def _sigmoid(x):
    return 1.0 / (1.0 + jnp.exp(-x))


def _rms(x, g):
    return x * lax.rsqrt(jnp.mean(x * x, axis=-1, keepdims=True) + EPS) * g


def _const_spec(shape):
    return pl.BlockSpec(shape, lambda *_: (0,) * len(shape))


def _in_proj_kernel(x_ref, g_ref, w_ref, q_ref, k_ref, v_ref, u_ref, gs_ref, gc_ref):
    xb = _rms(x_ref[...], g_ref[...]).astype(bf16)

    def proj(lo, width):
        return jnp.dot(xb, w_ref[:, lo:lo + width], preferred_element_type=f32)

    o = 0
    q_ref[...] = (proj(o, SB_WIDTH) * (SB_HEAD_DIM ** -0.5 * LOG2_E)).astype(bf16); o += SB_WIDTH
    k_ref[...] = proj(o, SB_WIDTH).astype(bf16); o += SB_WIDTH
    v_ref[...] = proj(o, SB_WIDTH).astype(bf16); o += SB_WIDTH
    a = proj(o, CONV_WIDTH); o += CONV_WIDTH
    b = proj(o, CONV_WIDTH); o += CONV_WIDTH
    u_ref[...] = (a * _sigmoid(b)).astype(bf16)
    d = gs_ref.shape[1]
    gs_ref[...] = _sigmoid(proj(o, d)).astype(bf16); o += d
    gc_ref[...] = _sigmoid(proj(o, d)).astype(bf16)


def _in_proj(x2, g, w_b, tm):
    t, d = x2.shape
    n_in = w_b.shape[1]
    row = lambda w: pl.BlockSpec((tm, w), lambda i: (i, 0))
    sds = lambda w: jax.ShapeDtypeStruct((t, w), bf16)
    return pl.pallas_call(
        _in_proj_kernel,
        grid=(t // tm,),
        in_specs=[row(d), _const_spec((1, d)), _const_spec((d, n_in))],
        out_specs=[row(SB_WIDTH), row(SB_WIDTH), row(SB_WIDTH), row(CONV_WIDTH), row(d), row(d)],
        out_shape=[sds(SB_WIDTH), sds(SB_WIDTH), sds(SB_WIDTH), sds(CONV_WIDTH), sds(d), sds(d)],
        compiler_params=pltpu.CompilerParams(
            dimension_semantics=("arbitrary",), vmem_limit_bytes=VMEM_LIMIT),
        name="in_proj",
    )(x2, g, w_b)


def _attn_kernel(q_ref, k_ref, v_ref, km_ref, vm_ref, tri_ref, o_ref, acc_ref, c_ref, *, tq, n_pairs):
    i = pl.program_id(1)
    lane = lax.broadcasted_iota(jnp.int32, (1, LANES), 1)
    low = lane < SB_HEAD_DIM

    def visit(load_kv, valid):
        for p in range(n_pairs):
            cols = slice(p * HEAD_PAIR, (p + 1) * HEAD_PAIR)
            kt, vt = load_kv(cols)
            tk = kt.shape[0]
            tri = tri_ref[:tk, :tk]
            q2 = q_ref[0, :, cols]
            zero = jnp.zeros_like(q2)
            qq = jnp.concatenate([jnp.where(low, q2, zero), jnp.where(low, zero, q2)], axis=0)
            z = lax.dot_general(qq, kt, (((1,), (1,)), ((), ())),
                                preferred_element_type=f32)
            sp = jnp.maximum(z, 0.0) + jnp.log2(1.0 + jnp.exp2(-jnp.abs(z)))
            if valid is not None:
                sp = jnp.where(valid, sp, 0.0)
            spb = sp.astype(bf16)
            c = c_ref[p]
            after = jnp.tile(c, (1, tk // LANES)) + jnp.dot(spb, tri, preferred_element_type=f32)
            a = jnp.exp2((z - sp) - after)
            if valid is not None:
                a = jnp.where(valid, a, 0.0)
            pv = jnp.dot(a.astype(bf16), vt, preferred_element_type=f32)
            c_new = after[:, 0:1] + spb[:, 0:1].astype(f32)
            c_ref[p] = jnp.broadcast_to(c_new, c.shape)
            acc_ref[:, cols] += jnp.where(low, pv[:tq], pv[tq:])

    def seq_tile(j):
        off = pl.multiple_of(j * tq, tq)
        return lambda cols: (k_ref[0, pl.ds(off, tq), cols], v_ref[0, pl.ds(off, tq), cols])

    def more_needed():
        return jnp.min(c_ref[...]) < STAY_SUM_EXIT

    acc_ref[...] = jnp.zeros_like(acc_ref)
    c_ref[...] = jnp.zeros_like(c_ref)

    qrow = lax.broadcasted_iota(jnp.int32, (2 * tq, tq), 0) & (tq - 1)
    kcol = lax.broadcasted_iota(jnp.int32, (2 * tq, tq), 1)
    visit(seq_tile(i), kcol < qrow)

    @pl.when(i > 0)
    def _():
        visit(seq_tile(i - 1), None)

    def cond(carry):
        j, more = carry
        return jnp.logical_and(j >= 0, more)

    def body(carry):
        j, _ = carry
        visit(seq_tile(j), None)
        return j - 1, more_needed()

    _, more = lax.while_loop(cond, body, (i - 2, more_needed()))

    @pl.when(more)
    def _():
        mcol = lax.broadcasted_iota(jnp.int32, (2 * tq, LANES), 1)
        visit(lambda cols: (km_ref[:, cols], vm_ref[:, cols]), mcol < N_META)

    o_ref[0] = acc_ref[...].astype(o_ref.dtype)


def _attention(q, k, v, km, vm, tq):
    b, s, w = q.shape
    n_pairs = w // HEAD_PAIR
    tri = jnp.asarray(np.tril(np.ones((tq, tq), np.float32), -1), bf16)
    seq = pl.BlockSpec((1, s, w), lambda bi, i: (bi, 0, 0))
    tile = pl.BlockSpec((1, tq, w), lambda bi, i: (bi, i, 0))
    return pl.pallas_call(
        functools.partial(_attn_kernel, tq=tq, n_pairs=n_pairs),
        grid=(b, s // tq),
        in_specs=[tile, seq, seq, _const_spec((LANES, w)), _const_spec((LANES, w)),
                  _const_spec((tq, tq))],
        out_specs=tile,
        out_shape=jax.ShapeDtypeStruct((b, s, w), bf16),
        scratch_shapes=[pltpu.VMEM((tq, w), f32), pltpu.VMEM((n_pairs, 2 * tq, LANES), f32)],
        compiler_params=pltpu.CompilerParams(
            dimension_semantics=("arbitrary", "arbitrary"), vmem_limit_bytes=VMEM_LIMIT),
        name="sb_attention",
    )(q, k, v, km, vm, tri)


def _mix_kernel(att_ref, u_ref, prev_ref, mhalo_ref, gs_ref, gc_ref, x_ref,
                wsb_ref, wcv_ref, wout_ref, cw_ref, cb_ref, lng_ref, lnb_ref, nf_ref,
                wrt_ref, brt_ref, upper_ref,
                h_ref, xn_ref, route_ref, cnt_out_ref,
                ubuf_ref, shift_ref, ycv_ref, cnt_ref, *, tm, tiles_per_seq, chunk):
    i = pl.program_id(0)
    first = (i % tiles_per_seq) == 0
    halo = jnp.where(first, mhalo_ref[...], prev_ref[...])
    ubuf_ref[0:CONV_HALO, :] = halo.astype(f32)
    ubuf_ref[CONV_HALO:, :] = u_ref[...].astype(f32)

    lead = CONV_HALO - (CONV_KERNEL - 1)
    span = shift_ref.shape[1]
    for ph in range(1, SUBLANES):
        shift_ref[ph - 1] = ubuf_ref[ph:ph + span, :]

    def tap(r0, kk):
        ph = (lead + kk) % SUBLANES
        base = r0 + lead + kk - ph
        if ph == 0:
            return ubuf_ref[base:base + chunk, :]
        return shift_ref[ph - 1, base:base + chunk, :]

    for r0 in range(0, tm, chunk):
        y = jnp.broadcast_to(cb_ref[...], (chunk, CONV_WIDTH))
        for kk in range(CONV_KERNEL):
            y = y + cw_ref[kk:kk + 1, :] * tap(r0, kk)
        mu = jnp.mean(y, axis=-1, keepdims=True)
        yc = y - mu
        var = jnp.mean(yc * yc, axis=-1, keepdims=True)
        yn = yc * lax.rsqrt(var + EPS) * lng_ref[...] + lnb_ref[...]
        ycv_ref[r0:r0 + chunk, :] = (yn * _sigmoid(yn)).astype(bf16)

    y_sb = jnp.dot(att_ref[...], wsb_ref[...], preferred_element_type=f32)
    y_cv = jnp.dot(ycv_ref[...], wcv_ref[...], preferred_element_type=f32)
    merged = gs_ref[...].astype(f32) * y_sb + gc_ref[...].astype(f32) * y_cv
    h = x_ref[...] + jnp.dot(merged.astype(bf16), wout_ref[...], preferred_element_type=f32)
    h_ref[...] = h
    xn = _rms(h, nf_ref[...])
    xnb = xn.astype(bf16)
    d_model = xn.shape[1]
    xn_ref[:, :d_model] = xn

    lt = lax.dot_general(wrt_ref[...], xnb, (((1,), (1,)), ((), ())),
                         preferred_element_type=f32) + brt_ref[...]
    gl = [lt[N_EXPERTS + g:N_EXPERTS + g + 1, :] for g in range(N_GROUPS)]
    gmax = functools.reduce(jnp.maximum, gl)
    gsum = functools.reduce(lambda a, b: a + b, [jnp.exp(x - gmax) for x in gl])
    g_p = 1.0 / gsum
    gidx = jnp.full_like(gmax, N_GROUPS - 1)
    for g in range(N_GROUPS - 2, -1, -1):
        gidx = jnp.where(gl[g] == gmax, float(g), gidx)
    el = []
    for j in range(EXPERTS_PER_GROUP):
        x = lt[(N_GROUPS - 1) * EXPERTS_PER_GROUP + j:(N_GROUPS - 1) * EXPERTS_PER_GROUP + j + 1, :]
        for g in range(N_GROUPS - 2, -1, -1):
            x = jnp.where(gidx == float(g), lt[g * EXPERTS_PER_GROUP + j:g * EXPERTS_PER_GROUP + j + 1, :], x)
        el.append(x)
    m1 = functools.reduce(jnp.maximum, el)
    i1 = jnp.full_like(m1, EXPERTS_PER_GROUP - 1)
    for j in range(EXPERTS_PER_GROUP - 2, -1, -1):
        i1 = jnp.where(el[j] == m1, float(j), i1)
    el2 = [jnp.where(i1 == float(j), -jnp.inf, el[j]) for j in range(EXPERTS_PER_GROUP)]
    m2 = functools.reduce(jnp.maximum, el2)
    i2 = jnp.full_like(m2, EXPERTS_PER_GROUP - 1)
    for j in range(EXPERTS_PER_GROUP - 2, -1, -1):
        i2 = jnp.where(el2[j] == m2, float(j), i2)
    w1 = g_p / (1.0 + jnp.exp(m2 - m1))
    w2 = g_p - w1
    lo = jnp.minimum(i1, i2)
    hi = jnp.maximum(i1, i2)
    pair = jnp.where(lo == 0.0, hi - 1.0, jnp.where(lo == 1.0, hi + 1.0, float(PAIRS_PER_GROUP - 1)))
    cls = gidx * PAIRS_PER_GROUP + pair
    first_is_lo = i1 < i2
    w_lo = jnp.where(first_is_lo, w1, w2)
    w_hi = jnp.where(first_is_lo, w2, w1)

    @pl.when(i == 0)
    def _():
        cnt_ref[...] = jnp.zeros_like(cnt_ref)

    in_cls = lax.broadcasted_iota(jnp.int32, (CLASS_ROWS, tm), 0).astype(f32) == cls
    member = jnp.where(in_cls, 1.0, 0.0).astype(bf16)
    before = jnp.dot(member, upper_ref[...], preferred_element_type=f32)
    cnt = cnt_ref[...]
    rank = jnp.sum(jnp.where(in_cls, before + jnp.tile(cnt, (1, tm // LANES)), 0.0),
                   axis=0, keepdims=True)
    cnt_ref[...] = cnt + jnp.dot(member, jnp.ones((tm, LANES), bf16), preferred_element_type=f32)
    cnt_out_ref[...] = cnt_ref[...]
    pad = jnp.zeros((6, tm), f32)
    route_ref[...] = jnp.concatenate([cls, rank, pad], axis=0).astype(jnp.int32)
    wrow = lax.broadcasted_iota(jnp.int32, (LANES, tm), 0)
    w_t = jnp.where(wrow == 0, w_lo, jnp.where(wrow == 1, w_hi, 0.0))
    xn_ref[:, d_model:] = w_t.T


def _mix(att, u, mhalo, gs, gc, x2, wsb, wcv, wout, cw, cb, lng, lnb, nf, wrt, brt, tm, seq_len):
    t, d = x2.shape
    tiles_per_seq = seq_len // tm
    row = lambda w: pl.BlockSpec((tm, w), lambda i: (i, 0))
    halo_blocks = tm // CONV_HALO
    prev = pl.BlockSpec((CONV_HALO, CONV_WIDTH), lambda i: (jnp.maximum(i * halo_blocks - 1, 0), 0))
    kern = functools.partial(_mix_kernel, tm=tm, tiles_per_seq=tiles_per_seq, chunk=64)
    upper = jnp.asarray(np.triu(np.ones((tm, tm), np.float32), 1), bf16)
    return pl.pallas_call(
        kern,
        grid=(t // tm,),
        in_specs=[row(SB_WIDTH), row(CONV_WIDTH), prev, _const_spec((CONV_HALO, CONV_WIDTH)),
                  row(d), row(d), row(d),
                  _const_spec((SB_WIDTH, d)), _const_spec((CONV_WIDTH, d)), _const_spec((d, d)),
                  _const_spec((CONV_KERNEL, CONV_WIDTH)), _const_spec((1, CONV_WIDTH)),
                  _const_spec((1, CONV_WIDTH)), _const_spec((1, CONV_WIDTH)), _const_spec((1, d)),
                  _const_spec((ROUTER_ROWS, d)), _const_spec((ROUTER_ROWS, 1)), _const_spec((tm, tm))],
        out_specs=[row(d), row(d + LANES), pl.BlockSpec((8, tm), lambda i: (0, i)),
                   _const_spec((CLASS_ROWS, LANES))],
        out_shape=[jax.ShapeDtypeStruct((t, d), f32), jax.ShapeDtypeStruct((t, d + LANES), f32),
                   jax.ShapeDtypeStruct((8, t), jnp.int32),
                   jax.ShapeDtypeStruct((CLASS_ROWS, LANES), f32)],
        scratch_shapes=[pltpu.VMEM((tm + CONV_HALO, CONV_WIDTH), f32),
                        pltpu.VMEM((SUBLANES - 1, tm + CONV_HALO - SUBLANES, CONV_WIDTH), f32),
                        pltpu.VMEM((tm, CONV_WIDTH), bf16),
                        pltpu.VMEM((CLASS_ROWS, LANES), f32)],
        compiler_params=pltpu.CompilerParams(
            dimension_semantics=("arbitrary",), vmem_limit_bytes=VMEM_LIMIT),
        name="mix",
    )(att, u, u, mhalo, gs, gc, x2, wsb, wcv, wout, cw, cb, lng, lnb, nf, wrt, brt, upper)


def _row_copy(src, src_row, dst, dst_row, sem):
    return pltpu.make_async_copy(src.at[pl.ds(src_row, 1)], dst.at[pl.ds(dst_row, 1)], sem)


def _scatter_kernel(zt_ref, pos_ref, x_ref, xs_hbm, xbuf, zbuf, sems, zsem, *, tm, te):
    i = pl.program_id(0)
    slot = i % 2

    @pl.when(i == 0)
    def _():
        zbuf[...] = jnp.zeros_like(zbuf)

        def zero_tile(k):
            start = pl.multiple_of(zt_ref[k] * te, te)
            return pltpu.make_async_copy(zbuf, xs_hbm.at[pl.ds(start, te)], zsem)

        for k in range(zt_ref.shape[0]):
            @pl.when(zt_ref[k] >= 0)
            def _():
                zero_tile(k).start()
        for k in range(zt_ref.shape[0]):
            @pl.when(zt_ref[k] >= 0)
            def _():
                zero_tile(k).wait()

    xbuf[slot] = x_ref[...]
    src = xbuf.at[slot]
    for r in range(tm):
        _row_copy(src, r, xs_hbm, pos_ref[0, 0, r], sems.at[slot]).start(priority=r % 2)

    def wait_tile(sl):
        pltpu.make_async_copy(xbuf.at[sl], xs_hbm.at[pl.ds(0, tm)], sems.at[sl]).wait()

    @pl.when(i > 0)
    def _():
        wait_tile(1 - slot)

    @pl.when(i == pl.num_programs(0) - 1)
    def _():
        wait_tile(slot)


def _scatter(zero_tiles, pos3, rows, n_rows, te):
    n_tiles, _, tm = pos3.shape
    w = rows.shape[1]
    grid_spec = pltpu.PrefetchScalarGridSpec(
        num_scalar_prefetch=1,
        grid=(n_tiles,),
        in_specs=[pl.BlockSpec((1, 1, tm), lambda i, zt: (i, 0, 0), memory_space=pltpu.SMEM),
                  pl.BlockSpec((tm, w), lambda i, zt: (i, 0))],
        out_specs=pl.BlockSpec(memory_space=pl.ANY),
        scratch_shapes=[pltpu.VMEM((2, tm, w), rows.dtype), pltpu.VMEM((te, w), rows.dtype),
                        pltpu.SemaphoreType.DMA((2,)), pltpu.SemaphoreType.DMA(())],
    )
    return pl.pallas_call(
        functools.partial(_scatter_kernel, tm=tm, te=te),
        grid_spec=grid_spec,
        out_shape=jax.ShapeDtypeStruct((n_rows, w), rows.dtype),
        compiler_params=pltpu.CompilerParams(
            dimension_semantics=("arbitrary",), vmem_limit_bytes=VMEM_LIMIT),
        name="moe_scatter",
    )(zero_tiles, pos3, rows)


def _experts_kernel(lo_ref, hi_ref, nu_ref, xs_ref, wgu_lo_ref, wd_lo_ref, wgu_hi_ref, wd_hi_ref,
                    ys_ref):
    del lo_ref, hi_ref
    i = pl.program_id(0)
    d = ys_ref.shape[1]

    @pl.when(i < nu_ref[0])
    def _():
        x = xs_ref[:, :d].astype(bf16)
        wts = xs_ref[:, d:]

        def scaled_hidden(wgu_ref, col):
            gu = jnp.dot(x, wgu_ref[0], preferred_element_type=f32)
            g = gu[:, :EXPERT_FF]
            return ((g * _sigmoid(g)) * gu[:, EXPERT_FF:] * wts[:, col:col + 1]).astype(bf16)

        ys_ref[...] = (
            jnp.dot(scaled_hidden(wgu_lo_ref, 0), wd_lo_ref[0], preferred_element_type=f32)
            + jnp.dot(scaled_hidden(wgu_hi_ref, 1), wd_hi_ref[0], preferred_element_type=f32))

    @pl.when(i >= nu_ref[0])
    def _():
        ys_ref[...] = jnp.zeros_like(ys_ref)


def _experts(tile_lo, tile_hi, n_used, xs, wgu, wd, tm):
    n_rows, w = xs.shape
    d = wd.shape[2]
    up = lambda which: pl.BlockSpec((1, d, 2 * EXPERT_FF), lambda i, lo, hi, nu: ((lo, hi)[which][i], 0, 0))
    down = lambda which: pl.BlockSpec((1, EXPERT_FF, d), lambda i, lo, hi, nu: ((lo, hi)[which][i], 0, 0))
    grid_spec = pltpu.PrefetchScalarGridSpec(
        num_scalar_prefetch=3,
        grid=(n_rows // tm,),
        in_specs=[pl.BlockSpec((tm, w), lambda i, lo, hi, nu: (i, 0)), up(0), down(0), up(1), down(1)],
        out_specs=pl.BlockSpec((tm, d), lambda i, lo, hi, nu: (i, 0)),
    )
    return pl.pallas_call(
        _experts_kernel,
        grid_spec=grid_spec,
        out_shape=jax.ShapeDtypeStruct((n_rows, d), f32),
        compiler_params=pltpu.CompilerParams(
            dimension_semantics=("arbitrary",), vmem_limit_bytes=VMEM_LIMIT),
        name="moe_experts",
    )(tile_lo, tile_hi, n_used, xs, wgu, wd, wgu, wd)


def _combine_kernel(pos_ref, nxt_ref, ys_hbm, h_ref, nf_ref, o_ref, gbuf, sems, *, tm):
    i = pl.program_id(0)
    slot = i % 2

    def issue(p_ref, sl):
        for r in range(tm):
            _row_copy(ys_hbm, p_ref[0, 0, r], gbuf.at[sl], r, sems.at[sl]).start(priority=r % 2)

    @pl.when(i == 0)
    def _():
        issue(pos_ref, 0)

    @pl.when(i + 1 < pl.num_programs(0))
    def _():
        issue(nxt_ref, 1 - slot)

    pltpu.make_async_copy(ys_hbm.at[pl.ds(0, tm)], gbuf.at[slot], sems.at[slot]).wait()
    o_ref[...] = _rms(h_ref[...] + gbuf[slot], nf_ref[...])


def _combine(pos3, ys, h, nf):
    n_tiles, _, tm = pos3.shape
    t, d = h.shape
    row = pl.BlockSpec((tm, d), lambda i: (i, 0))
    last = n_tiles - 1
    return pl.pallas_call(
        functools.partial(_combine_kernel, tm=tm),
        grid=(n_tiles,),
        in_specs=[pl.BlockSpec((1, 1, tm), lambda i: (i, 0, 0), memory_space=pltpu.SMEM),
                  pl.BlockSpec((1, 1, tm), lambda i: (jnp.minimum(i + 1, last), 0, 0),
                               memory_space=pltpu.SMEM),
                  pl.BlockSpec(memory_space=pl.ANY), row, _const_spec((1, d))],
        out_specs=row,
        out_shape=jax.ShapeDtypeStruct((t, d), f32),
        scratch_shapes=[pltpu.VMEM((2, tm, d), f32), pltpu.SemaphoreType.DMA((2,))],
        compiler_params=pltpu.CompilerParams(
            dimension_semantics=("arbitrary",), vmem_limit_bytes=VMEM_LIMIT),
        name="moe_combine",
    )(pos3, pos3, ys, h, nf)


def _tile_positions(pos, tm):
    return pos.reshape(-1, 1, tm)


def kernel(x, meta, norm_mix, w_in, w_sb_o, conv_w, conv_b, conv_ln_g, conv_ln_b, w_conv_o, w_out,
           norm_ffn, w_router_group, b_router_group, w_router_expert, b_router_expert, w_gate, w_up,
           w_down, norm_final):
    assert norm_mix.shape[0] == 1, "single-layer block"
    b, s, d = x.shape
    t = b * s
    tm = min(512, s)
    tq = min(256, s)
    x2 = x.reshape(t, d)
    w_in_b = w_in[0].astype(bf16)
    g_mix = norm_mix[0][None]

    q, k, v, u, gs, gc = _in_proj(x2, g_mix, w_in_b, tm)
    _, km, vm, um, _, _ = _in_proj(meta, g_mix, w_in_b, N_META)
    pad_rows = lambda a, top, bottom: jnp.pad(a, ((top, bottom), (0, 0)))
    km = pad_rows(km, 0, LANES - N_META)
    vm = pad_rows(vm, 0, LANES - N_META)
    shp = (b, s, SB_WIDTH)
    att = _attention(q.reshape(shp), k.reshape(shp), v.reshape(shp), km, vm, tq).reshape(t, SB_WIDTH)

    mhalo = pad_rows(um, CONV_HALO - N_META, 0)
    wrt = jnp.concatenate([w_router_expert[0], w_router_group[0]], axis=1).T
    wrt = pad_rows(wrt, 0, ROUTER_ROWS - wrt.shape[0]).astype(bf16)
    brt = jnp.concatenate([b_router_expert[0], b_router_group[0]])[:, None]
    brt = pad_rows(brt, 0, ROUTER_ROWS - brt.shape[0]).astype(f32)
    h, xrow, route, counts = _mix(
        att, u, mhalo, gs, gc, x2,
        w_sb_o[0].astype(bf16), w_conv_o[0].astype(bf16), w_out[0].astype(bf16),
        conv_w[0], conv_b[0][None], conv_ln_g[0][None], conv_ln_b[0][None],
        norm_ffn[0][None], wrt, brt, tm, s)

    te = min(512, s)
    n_rows = t + N_CLASSES * te
    cnt = counts[:N_CLASSES, 0].astype(jnp.int32)
    tiles = (cnt + te - 1) // te
    tile_end = jnp.cumsum(tiles)
    row_start = (tile_end - tiles) * te
    classes = jnp.arange(N_CLASSES, dtype=jnp.int32)
    pos = jnp.sum(jnp.where(route[0][:, None] == classes, row_start, 0), axis=-1) + route[1]
    tile_ids = jnp.arange(n_rows // te, dtype=jnp.int32)
    tile_cls = jnp.minimum(jnp.sum(tile_ids[:, None] >= tile_end, axis=-1), N_CLASSES - 1)
    cls_hot = tile_cls[:, None] == classes
    lo_of = jnp.array([c // PAIRS_PER_GROUP * EXPERTS_PER_GROUP + GROUP_PAIRS[c % PAIRS_PER_GROUP][0]
                       for c in range(N_CLASSES)], jnp.int32)
    hi_of = jnp.array([c // PAIRS_PER_GROUP * EXPERTS_PER_GROUP + GROUP_PAIRS[c % PAIRS_PER_GROUP][1]
                       for c in range(N_CLASSES)], jnp.int32)
    tile_lo = jnp.sum(jnp.where(cls_hot, lo_of, 0), axis=-1).astype(jnp.int32)
    tile_hi = jnp.sum(jnp.where(cls_hot, hi_of, 0), axis=-1).astype(jnp.int32)
    n_used = tile_end[-1:].astype(jnp.int32)
    last_tile = jnp.where(tiles > 0, tile_end - 1, -1)
    tail = n_used + classes
    zero_tiles = jnp.concatenate([last_tile, jnp.where(tail < n_rows // te, tail, -1)]).astype(jnp.int32)

    xs = _scatter(zero_tiles, _tile_positions(pos, min(512, s)), xrow, n_rows, te)
    wgu = jnp.concatenate([w_gate[0], w_up[0]], axis=-1).astype(bf16)
    ys = _experts(tile_lo, tile_hi, n_used, xs, wgu, w_down[0].astype(bf16), te)
    out = _combine(_tile_positions(pos, min(256, s)), ys, h, norm_final[None])
    return out.reshape(b, s, d)
```

```python
import functools
from typing import NamedTuple

import jax
import jax.numpy as jnp
import numpy as np
from jax import lax
from jax.experimental import pallas as pl
from jax.experimental.pallas import tpu as pltpu

N_META = 16
SB_HEAD_DIM = 64
SB_WIDTH = 512
CONV_WIDTH = 512
CONV_KERNEL = 31
N_GROUPS = 4
EXPERTS_PER_GROUP = 4
N_EXPERTS = N_GROUPS * EXPERTS_PER_GROUP
EXPERT_FF = 512
GROUP_PAIRS = [(a, b) for a in range(EXPERTS_PER_GROUP) for b in range(a + 1, EXPERTS_PER_GROUP)]
PAIRS_PER_GROUP = len(GROUP_PAIRS)
N_CLASSES = N_GROUPS * PAIRS_PER_GROUP
CLASS_ROWS = 32
EPS = 1e-6

LANES = 128
SUBLANES = 8
HEAD_PAIR = 2 * SB_HEAD_DIM
CONV_HALO = 32
ROUTER_ROWS = 32
STAY_SUM_EXIT = 150.0
LOG2_E = 1.4426950408889634
VMEM_LIMIT = 56 * 1024 * 1024

f32 = jnp.float32
bf16 = jnp.bfloat16


def _sigmoid(x):
    return 1.0 / (1.0 + jnp.exp(-x))


def _rms(x, g):
    return x * lax.rsqrt(jnp.mean(x * x, axis=-1, keepdims=True) + EPS) * g


def _const_spec(shape):
    return pl.BlockSpec(shape, lambda *_: (0,) * len(shape))


def _in_proj_kernel(x_ref, g_ref, w_ref, q_ref, k_ref, v_ref, u_ref, gs_ref, gc_ref):
    xb = _rms(x_ref[...], g_ref[...]).astype(bf16)

    def proj(lo, width):
        return jnp.dot(xb, w_ref[:, lo:lo + width], preferred_element_type=f32)

    o = 0
    q_ref[...] = (proj(o, SB_WIDTH) * (SB_HEAD_DIM ** -0.5 * LOG2_E)).astype(bf16); o += SB_WIDTH
    k_ref[...] = proj(o, SB_WIDTH).astype(bf16); o += SB_WIDTH
    v_ref[...] = proj(o, SB_WIDTH).astype(bf16); o += SB_WIDTH
    a = proj(o, CONV_WIDTH); o += CONV_WIDTH
    b = proj(o, CONV_WIDTH); o += CONV_WIDTH
    u_ref[...] = (a * _sigmoid(b)).astype(bf16)
    d = gs_ref.shape[1]
    gs_ref[...] = _sigmoid(proj(o, d)).astype(bf16); o += d
    gc_ref[...] = _sigmoid(proj(o, d)).astype(bf16)


def _in_proj(x2, g, w_b, tm):
    t, d = x2.shape
    n_in = w_b.shape[1]
    row = lambda w: pl.BlockSpec((tm, w), lambda i: (i, 0))
    sds = lambda w: jax.ShapeDtypeStruct((t, w), bf16)
    return pl.pallas_call(
        _in_proj_kernel,
        grid=(t // tm,),
        in_specs=[row(d), _const_spec((1, d)), _const_spec((d, n_in))],
        out_specs=[row(SB_WIDTH), row(SB_WIDTH), row(SB_WIDTH), row(CONV_WIDTH), row(d), row(d)],
        out_shape=[sds(SB_WIDTH), sds(SB_WIDTH), sds(SB_WIDTH), sds(CONV_WIDTH), sds(d), sds(d)],
        compiler_params=pltpu.CompilerParams(
            dimension_semantics=("arbitrary",), vmem_limit_bytes=VMEM_LIMIT),
        name="in_proj",
    )(x2, g, w_b)


def _attn_kernel(q_ref, k_ref, v_ref, km_ref, vm_ref, tri_ref, o_ref, acc_ref, c_ref, *, tq, n_pairs):
    i = pl.program_id(1)
    lane = lax.broadcasted_iota(jnp.int32, (1, LANES), 1)
    low = lane < SB_HEAD_DIM

    def visit(load_kv, valid):
        for p in range(n_pairs):
            cols = slice(p * HEAD_PAIR, (p + 1) * HEAD_PAIR)
            kt, vt = load_kv(cols)
            tk = kt.shape[0]
            tri = tri_ref[:tk, :tk]
            q2 = q_ref[0, :, cols]
            zero = jnp.zeros_like(q2)
            qq = jnp.concatenate([jnp.where(low, q2, zero), jnp.where(low, zero, q2)], axis=0)
            z = lax.dot_general(qq, kt, (((1,), (1,)), ((), ())),
                                preferred_element_type=f32)
            sp = jnp.maximum(z, 0.0) + jnp.log2(1.0 + jnp.exp2(-jnp.abs(z)))
            if valid is not None:
                sp = jnp.where(valid, sp, 0.0)
            spb = sp.astype(bf16)
            c = c_ref[p]
            after = jnp.tile(c, (1, tk // LANES)) + jnp.dot(spb, tri, preferred_element_type=f32)
            a = jnp.exp2((z - sp) - after)
            if valid is not None:
                a = jnp.where(valid, a, 0.0)
            pv = jnp.dot(a.astype(bf16), vt, preferred_element_type=f32)
            c_new = after[:, 0:1] + spb[:, 0:1].astype(f32)
            c_ref[p] = jnp.broadcast_to(c_new, c.shape)
            acc_ref[:, cols] += jnp.where(low, pv[:tq], pv[tq:])

    def seq_tile(j):
        off = pl.multiple_of(j * tq, tq)
        return lambda cols: (k_ref[0, pl.ds(off, tq), cols], v_ref[0, pl.ds(off, tq), cols])

    def more_needed():
        return jnp.min(c_ref[...]) < STAY_SUM_EXIT

    acc_ref[...] = jnp.zeros_like(acc_ref)
    c_ref[...] = jnp.zeros_like(c_ref)

    qrow = lax.broadcasted_iota(jnp.int32, (2 * tq, tq), 0) & (tq - 1)
    kcol = lax.broadcasted_iota(jnp.int32, (2 * tq, tq), 1)
    visit(seq_tile(i), kcol < qrow)

    @pl.when(i > 0)
    def _():
        visit(seq_tile(i - 1), None)

    def cond(carry):
        j, more = carry
        return jnp.logical_and(j >= 0, more)

    def body(carry):
        j, _ = carry
        visit(seq_tile(j), None)
        return j - 1, more_needed()

    _, more = lax.while_loop(cond, body, (i - 2, more_needed()))

    @pl.when(more)
    def _():
        mcol = lax.broadcasted_iota(jnp.int32, (2 * tq, LANES), 1)
        visit(lambda cols: (km_ref[:, cols], vm_ref[:, cols]), mcol < N_META)

    o_ref[0] = acc_ref[...].astype(o_ref.dtype)


def _attention(q, k, v, km, vm, tq):
    b, s, w = q.shape
    n_pairs = w // HEAD_PAIR
    tri = jnp.asarray(np.tril(np.ones((tq, tq), np.float32), -1), bf16)
    seq = pl.BlockSpec((1, s, w), lambda bi, i: (bi, 0, 0))
    tile = pl.BlockSpec((1, tq, w), lambda bi, i: (bi, i, 0))
    return pl.pallas_call(
        functools.partial(_attn_kernel, tq=tq, n_pairs=n_pairs),
        grid=(b, s // tq),
        in_specs=[tile, seq, seq, _const_spec((LANES, w)), _const_spec((LANES, w)),
                  _const_spec((tq, tq))],
        out_specs=tile,
        out_shape=jax.ShapeDtypeStruct((b, s, w), bf16),
        scratch_shapes=[pltpu.VMEM((tq, w), f32), pltpu.VMEM((n_pairs, 2 * tq, LANES), f32)],
        compiler_params=pltpu.CompilerParams(
            dimension_semantics=("arbitrary", "arbitrary"), vmem_limit_bytes=VMEM_LIMIT),
        name="sb_attention",
    )(q, k, v, km, vm, tri)


def _mix_kernel(att_ref, u_ref, prev_ref, mhalo_ref, gs_ref, gc_ref, x_ref,
                wsb_ref, wcv_ref, wout_ref, cw_ref, cb_ref, lng_ref, lnb_ref, nf_ref,
                wrt_ref, brt_ref, upper_ref,
                h_ref, xn_ref, route_ref, cnt_out_ref,
                ubuf_ref, shift_ref, ycv_ref, cnt_ref, *, tm, tiles_per_seq, chunk):
    i = pl.program_id(0)
    first = (i % tiles_per_seq) == 0
    halo = jnp.where(first, mhalo_ref[...], prev_ref[...])
    ubuf_ref[0:CONV_HALO, :] = halo.astype(f32)
    ubuf_ref[CONV_HALO:, :] = u_ref[...].astype(f32)

    lead = CONV_HALO - (CONV_KERNEL - 1)
    span = shift_ref.shape[1]
    for ph in range(1, SUBLANES):
        shift_ref[ph - 1] = ubuf_ref[ph:ph + span, :]

    def tap(r0, kk):
        ph = (lead + kk) % SUBLANES
        base = r0 + lead + kk - ph
        if ph == 0:
            return ubuf_ref[base:base + chunk, :]
        return shift_ref[ph - 1, base:base + chunk, :]

    for r0 in range(0, tm, chunk):
        y = jnp.broadcast_to(cb_ref[...], (chunk, CONV_WIDTH))
        for kk in range(CONV_KERNEL):
            y = y + cw_ref[kk:kk + 1, :] * tap(r0, kk)
        mu = jnp.mean(y, axis=-1, keepdims=True)
        yc = y - mu
        var = jnp.mean(yc * yc, axis=-1, keepdims=True)
        yn = yc * lax.rsqrt(var + EPS) * lng_ref[...] + lnb_ref[...]
        ycv_ref[r0:r0 + chunk, :] = (yn * _sigmoid(yn)).astype(bf16)

    y_sb = jnp.dot(att_ref[...], wsb_ref[...], preferred_element_type=f32)
    y_cv = jnp.dot(ycv_ref[...], wcv_ref[...], preferred_element_type=f32)
    merged = gs_ref[...].astype(f32) * y_sb + gc_ref[...].astype(f32) * y_cv
    h = x_ref[...] + jnp.dot(merged.astype(bf16), wout_ref[...], preferred_element_type=f32)
    h_ref[...] = h
    xn = _rms(h, nf_ref[...])
    xnb = xn.astype(bf16)
    d_model = xn.shape[1]
    xn_ref[:, :d_model] = xn

    lt = lax.dot_general(wrt_ref[...], xnb, (((1,), (1,)), ((), ())),
                         preferred_element_type=f32) + brt_ref[...]
    gl = [lt[N_EXPERTS + g:N_EXPERTS + g + 1, :] for g in range(N_GROUPS)]
    gmax = functools.reduce(jnp.maximum, gl)
    gsum = functools.reduce(lambda a, b: a + b, [jnp.exp(x - gmax) for x in gl])
    g_p = 1.0 / gsum
    gidx = jnp.full_like(gmax, N_GROUPS - 1)
    for g in range(N_GROUPS - 2, -1, -1):
        gidx = jnp.where(gl[g] == gmax, float(g), gidx)
    el = []
    for j in range(EXPERTS_PER_GROUP):
        x = lt[(N_GROUPS - 1) * EXPERTS_PER_GROUP + j:(N_GROUPS - 1) * EXPERTS_PER_GROUP + j + 1, :]
        for g in range(N_GROUPS - 2, -1, -1):
            x = jnp.where(gidx == float(g), lt[g * EXPERTS_PER_GROUP + j:g * EXPERTS_PER_GROUP + j + 1, :], x)
        el.append(x)
    m1 = functools.reduce(jnp.maximum, el)
    i1 = jnp.full_like(m1, EXPERTS_PER_GROUP - 1)
    for j in range(EXPERTS_PER_GROUP - 2, -1, -1):
        i1 = jnp.where(el[j] == m1, float(j), i1)
    el2 = [jnp.where(i1 == float(j), -jnp.inf, el[j]) for j in range(EXPERTS_PER_GROUP)]
    m2 = functools.reduce(jnp.maximum, el2)
    i2 = jnp.full_like(m2, EXPERTS_PER_GROUP - 1)
    for j in range(EXPERTS_PER_GROUP - 2, -1, -1):
        i2 = jnp.where(el2[j] == m2, float(j), i2)
    w1 = g_p / (1.0 + jnp.exp(m2 - m1))
    w2 = g_p - w1
    lo = jnp.minimum(i1, i2)
    hi = jnp.maximum(i1, i2)
    pair = jnp.where(lo == 0.0, hi - 1.0, jnp.where(lo == 1.0, hi + 1.0, float(PAIRS_PER_GROUP - 1)))
    cls = gidx * PAIRS_PER_GROUP + pair
    first_is_lo = i1 < i2
    w_lo = jnp.where(first_is_lo, w1, w2)
    w_hi = jnp.where(first_is_lo, w2, w1)

    @pl.when(i == 0)
    def _():
        cnt_ref[...] = jnp.zeros_like(cnt_ref)

    in_cls = lax.broadcasted_iota(jnp.int32, (CLASS_ROWS, tm), 0).astype(f32) == cls
    member = jnp.where(in_cls, 1.0, 0.0).astype(bf16)
    before = jnp.dot(member, upper_ref[...], preferred_element_type=f32)
    cnt = cnt_ref[...]
    rank = jnp.sum(jnp.where(in_cls, before + jnp.tile(cnt, (1, tm // LANES)), 0.0),
                   axis=0, keepdims=True)
    cnt_ref[...] = cnt + jnp.dot(member, jnp.ones((tm, LANES), bf16), preferred_element_type=f32)
    cnt_out_ref[...] = cnt_ref[...]
    pad = jnp.zeros((6, tm), f32)
    route_ref[...] = jnp.concatenate([cls, rank, pad], axis=0).astype(jnp.int32)
    wrow = lax.broadcasted_iota(jnp.int32, (LANES, tm), 0)
    w_t = jnp.where(wrow == 0, w_lo, jnp.where(wrow == 1, w_hi, 0.0))
    xn_ref[:, d_model:] = w_t.T


def _mix(att, u, mhalo, gs, gc, x2, wsb, wcv, wout, cw, cb, lng, lnb, nf, wrt, brt, tm, seq_len):
    t, d = x2.shape
    tiles_per_seq = seq_len // tm
    row = lambda w: pl.BlockSpec((tm, w), lambda i: (i, 0))
    halo_blocks = tm // CONV_HALO
    prev = pl.BlockSpec((CONV_HALO, CONV_WIDTH), lambda i: (jnp.maximum(i * halo_blocks - 1, 0), 0))
    kern = functools.partial(_mix_kernel, tm=tm, tiles_per_seq=tiles_per_seq, chunk=64)
    upper = jnp.asarray(np.triu(np.ones((tm, tm), np.float32), 1), bf16)
    return pl.pallas_call(
        kern,
        grid=(t // tm,),
        in_specs=[row(SB_WIDTH), row(CONV_WIDTH), prev, _const_spec((CONV_HALO, CONV_WIDTH)),
                  row(d), row(d), row(d),
                  _const_spec((SB_WIDTH, d)), _const_spec((CONV_WIDTH, d)), _const_spec((d, d)),
                  _const_spec((CONV_KERNEL, CONV_WIDTH)), _const_spec((1, CONV_WIDTH)),
                  _const_spec((1, CONV_WIDTH)), _const_spec((1, CONV_WIDTH)), _const_spec((1, d)),
                  _const_spec((ROUTER_ROWS, d)), _const_spec((ROUTER_ROWS, 1)), _const_spec((tm, tm))],
        out_specs=[row(d), row(d + LANES), pl.BlockSpec((8, tm), lambda i: (0, i)),
                   _const_spec((CLASS_ROWS, LANES))],
        out_shape=[jax.ShapeDtypeStruct((t, d), f32), jax.ShapeDtypeStruct((t, d + LANES), f32),
                   jax.ShapeDtypeStruct((8, t), jnp.int32),
                   jax.ShapeDtypeStruct((CLASS_ROWS, LANES), f32)],
        scratch_shapes=[pltpu.VMEM((tm + CONV_HALO, CONV_WIDTH), f32),
                        pltpu.VMEM((SUBLANES - 1, tm + CONV_HALO - SUBLANES, CONV_WIDTH), f32),
                        pltpu.VMEM((tm, CONV_WIDTH), bf16),
                        pltpu.VMEM((CLASS_ROWS, LANES), f32)],
        compiler_params=pltpu.CompilerParams(
            dimension_semantics=("arbitrary",), vmem_limit_bytes=VMEM_LIMIT),
        name="mix",
    )(att, u, u, mhalo, gs, gc, x2, wsb, wcv, wout, cw, cb, lng, lnb, nf, wrt, brt, upper)


def _row_copy(src, src_row, dst, dst_row, sem):
    return pltpu.make_async_copy(src.at[pl.ds(src_row, 1)], dst.at[pl.ds(dst_row, 1)], sem)


def _scatter_kernel(zt_ref, pos_ref, x_ref, xs_hbm, xbuf, zbuf, sems, zsem, *, tm, te):
    i = pl.program_id(0)
    slot = i % 2

    @pl.when(i == 0)
    def _():
        zbuf[...] = jnp.zeros_like(zbuf)

        def zero_tile(k):
            start = pl.multiple_of(zt_ref[k] * te, te)
            return pltpu.make_async_copy(zbuf, xs_hbm.at[pl.ds(start, te)], zsem)

        for k in range(zt_ref.shape[0]):
            @pl.when(zt_ref[k] >= 0)
            def _():
                zero_tile(k).start()
        for k in range(zt_ref.shape[0]):
            @pl.when(zt_ref[k] >= 0)
            def _():
                zero_tile(k).wait()

    xbuf[slot] = x_ref[...]
    src = xbuf.at[slot]
    for r in range(tm):
        _row_copy(src, r, xs_hbm, pos_ref[0, 0, r], sems.at[slot]).start(priority=r % 2)

    def wait_tile(sl):
        pltpu.make_async_copy(xbuf.at[sl], xs_hbm.at[pl.ds(0, tm)], sems.at[sl]).wait()

    @pl.when(i > 0)
    def _():
        wait_tile(1 - slot)

    @pl.when(i == pl.num_programs(0) - 1)
    def _():
        wait_tile(slot)


def _scatter(zero_tiles, pos3, rows, n_rows, te):
    n_tiles, _, tm = pos3.shape
    w = rows.shape[1]
    grid_spec = pltpu.PrefetchScalarGridSpec(
        num_scalar_prefetch=1,
        grid=(n_tiles,),
        in_specs=[pl.BlockSpec((1, 1, tm), lambda i, zt: (i, 0, 0), memory_space=pltpu.SMEM),
                  pl.BlockSpec((tm, w), lambda i, zt: (i, 0))],
        out_specs=pl.BlockSpec(memory_space=pl.ANY),
        scratch_shapes=[pltpu.VMEM((2, tm, w), rows.dtype), pltpu.VMEM((te, w), rows.dtype),
                        pltpu.SemaphoreType.DMA((2,)), pltpu.SemaphoreType.DMA(())],
    )
    return pl.pallas_call(
        functools.partial(_scatter_kernel, tm=tm, te=te),
        grid_spec=grid_spec,
        out_shape=jax.ShapeDtypeStruct((n_rows, w), rows.dtype),
        compiler_params=pltpu.CompilerParams(
            dimension_semantics=("arbitrary",), vmem_limit_bytes=VMEM_LIMIT),
        name="moe_scatter",
    )(zero_tiles, pos3, rows)


def _experts_kernel(lo_ref, hi_ref, nu_ref, xs_ref, wgu_lo_ref, wd_lo_ref, wgu_hi_ref, wd_hi_ref,
                    ys_ref):
    del lo_ref, hi_ref
    i = pl.program_id(0)
    d = ys_ref.shape[1]

    @pl.when(i < nu_ref[0])
    def _():
        x = xs_ref[:, :d].astype(bf16)
        wts = xs_ref[:, d:]

        def scaled_hidden(wgu_ref, col):
            gu = jnp.dot(x, wgu_ref[0], preferred_element_type=f32)
            g = gu[:, :EXPERT_FF]
            return ((g * _sigmoid(g)) * gu[:, EXPERT_FF:] * wts[:, col:col + 1]).astype(bf16)

        ys_ref[...] = (
            jnp.dot(scaled_hidden(wgu_lo_ref, 0), wd_lo_ref[0], preferred_element_type=f32)
            + jnp.dot(scaled_hidden(wgu_hi_ref, 1), wd_hi_ref[0], preferred_element_type=f32))

    @pl.when(i >= nu_ref[0])
    def _():
        ys_ref[...] = jnp.zeros_like(ys_ref)


def _experts(tile_lo, tile_hi, n_used, xs, wgu, wd, tm):
    n_rows, w = xs.shape
    d = wd.shape[2]
    up = lambda which: pl.BlockSpec((1, d, 2 * EXPERT_FF), lambda i, lo, hi, nu: ((lo, hi)[which][i], 0, 0))
    down = lambda which: pl.BlockSpec((1, EXPERT_FF, d), lambda i, lo, hi, nu: ((lo, hi)[which][i], 0, 0))
    grid_spec = pltpu.PrefetchScalarGridSpec(
        num_scalar_prefetch=3,
        grid=(n_rows // tm,),
        in_specs=[pl.BlockSpec((tm, w), lambda i, lo, hi, nu: (i, 0)), up(0), down(0), up(1), down(1)],
        out_specs=pl.BlockSpec((tm, d), lambda i, lo, hi, nu: (i, 0)),
    )
    return pl.pallas_call(
        _experts_kernel,
        grid_spec=grid_spec,
        out_shape=jax.ShapeDtypeStruct((n_rows, d), f32),
        compiler_params=pltpu.CompilerParams(
            dimension_semantics=("arbitrary",), vmem_limit_bytes=VMEM_LIMIT),
        name="moe_experts",
    )(tile_lo, tile_hi, n_used, xs, wgu, wd, wgu, wd)


def _combine_kernel(pos_ref, nxt_ref, ys_hbm, h_ref, nf_ref, o_ref, gbuf, sems, *, tm):
    i = pl.program_id(0)
    slot = i % 2

    def issue(p_ref, sl):
        for r in range(tm):
            _row_copy(ys_hbm, p_ref[0, 0, r], gbuf.at[sl], r, sems.at[sl]).start(priority=r % 2)

    @pl.when(i == 0)
    def _():
        issue(pos_ref, 0)

    @pl.when(i + 1 < pl.num_programs(0))
    def _():
        issue(nxt_ref, 1 - slot)

    pltpu.make_async_copy(ys_hbm.at[pl.ds(0, tm)], gbuf.at[slot], sems.at[slot]).wait()
    o_ref[...] = _rms(h_ref[...] + gbuf[slot], nf_ref[...])


def _combine(pos3, ys, h, nf):
    n_tiles, _, tm = pos3.shape
    t, d = h.shape
    row = pl.BlockSpec((tm, d), lambda i: (i, 0))
    last = n_tiles - 1
    return pl.pallas_call(
        functools.partial(_combine_kernel, tm=tm),
        grid=(n_tiles,),
        in_specs=[pl.BlockSpec((1, 1, tm), lambda i: (i, 0, 0), memory_space=pltpu.SMEM),
                  pl.BlockSpec((1, 1, tm), lambda i: (jnp.minimum(i + 1, last), 0, 0),
                               memory_space=pltpu.SMEM),
                  pl.BlockSpec(memory_space=pl.ANY), row, _const_spec((1, d))],
        out_specs=row,
        out_shape=jax.ShapeDtypeStruct((t, d), f32),
        scratch_shapes=[pltpu.VMEM((2, tm, d), f32), pltpu.SemaphoreType.DMA((2,))],
        compiler_params=pltpu.CompilerParams(
            dimension_semantics=("arbitrary",), vmem_limit_bytes=VMEM_LIMIT),
        name="moe_combine",
    )(pos3, pos3, ys, h, nf)


def _tile_positions(pos, tm):
    return pos.reshape(-1, 1, tm)


class _Tiles(NamedTuple):
    rows: int
    query: int
    expert: int
    scatter: int
    combine: int

    @classmethod
    def plan(cls, seq_len):
        cap = lambda n: min(n, seq_len)
        return cls(rows=cap(512), query=cap(256), expert=cap(512), scatter=cap(1024), combine=cap(512))


def kernel(x, meta, norm_mix, w_in, w_sb_o, conv_w, conv_b, conv_ln_g, conv_ln_b, w_conv_o, w_out,
           norm_ffn, w_router_group, b_router_group, w_router_expert, b_router_expert, w_gate, w_up,
           w_down, norm_final):
    assert norm_mix.shape[0] == 1, "single-layer block"
    b, s, d = x.shape
    t = b * s
    tiles_of = _Tiles.plan(s)
    tm, tq, te = tiles_of.rows, tiles_of.query, tiles_of.expert
    x2 = x.reshape(t, d)
    w_in_b = w_in[0].astype(bf16)
    g_mix = norm_mix[0][None]

    q, k, v, u, gs, gc = _in_proj(x2, g_mix, w_in_b, tm)
    _, km, vm, um, _, _ = _in_proj(meta, g_mix, w_in_b, N_META)
    pad_rows = lambda a, top, bottom: jnp.pad(a, ((top, bottom), (0, 0)))
    km = pad_rows(km, 0, LANES - N_META)
    vm = pad_rows(vm, 0, LANES - N_META)
    shp = (b, s, SB_WIDTH)
    att = _attention(q.reshape(shp), k.reshape(shp), v.reshape(shp), km, vm, tq).reshape(t, SB_WIDTH)

    mhalo = pad_rows(um, CONV_HALO - N_META, 0)
    wrt = jnp.concatenate([w_router_expert[0], w_router_group[0]], axis=1).T
    wrt = pad_rows(wrt, 0, ROUTER_ROWS - wrt.shape[0]).astype(bf16)
    brt = jnp.concatenate([b_router_expert[0], b_router_group[0]])[:, None]
    brt = pad_rows(brt, 0, ROUTER_ROWS - brt.shape[0]).astype(f32)
    h, xrow, route, counts = _mix(
        att, u, mhalo, gs, gc, x2,
        w_sb_o[0].astype(bf16), w_conv_o[0].astype(bf16), w_out[0].astype(bf16),
        conv_w[0], conv_b[0][None], conv_ln_g[0][None], conv_ln_b[0][None],
        norm_ffn[0][None], wrt, brt, tm, s)

    n_rows = t + N_CLASSES * te
    cnt = counts[:N_CLASSES, 0].astype(jnp.int32)
    tiles = (cnt + te - 1) // te
    tile_end = jnp.cumsum(tiles)
    row_start = (tile_end - tiles) * te
    classes = jnp.arange(N_CLASSES, dtype=jnp.int32)
    pos = jnp.sum(jnp.where(route[0][:, None] == classes, row_start, 0), axis=-1) + route[1]
    tile_ids = jnp.arange(n_rows // te, dtype=jnp.int32)
    tile_cls = jnp.minimum(jnp.sum(tile_ids[:, None] >= tile_end, axis=-1), N_CLASSES - 1)
    cls_hot = tile_cls[:, None] == classes
    lo_of = jnp.array([c // PAIRS_PER_GROUP * EXPERTS_PER_GROUP + GROUP_PAIRS[c % PAIRS_PER_GROUP][0]
                       for c in range(N_CLASSES)], jnp.int32)
    hi_of = jnp.array([c // PAIRS_PER_GROUP * EXPERTS_PER_GROUP + GROUP_PAIRS[c % PAIRS_PER_GROUP][1]
                       for c in range(N_CLASSES)], jnp.int32)
    tile_lo = jnp.sum(jnp.where(cls_hot, lo_of, 0), axis=-1).astype(jnp.int32)
    tile_hi = jnp.sum(jnp.where(cls_hot, hi_of, 0), axis=-1).astype(jnp.int32)
    n_used = tile_end[-1:].astype(jnp.int32)
    last_tile = jnp.where(tiles > 0, tile_end - 1, -1)
    tail = n_used + classes
    zero_tiles = jnp.concatenate([last_tile, jnp.where(tail < n_rows // te, tail, -1)]).astype(jnp.int32)

    xs = _scatter(zero_tiles, _tile_positions(pos, tiles_of.scatter), xrow, n_rows, te)
    wgu = jnp.concatenate([w_gate[0], w_up[0]], axis=-1).astype(bf16)
    ys = _experts(tile_lo, tile_hi, n_used, xs, wgu, w_down[0].astype(bf16), te)
    out = _combine(_tile_positions(pos, tiles_of.combine), ys, h, norm_final[None])
    return out.reshape(b, s, d)
```

```python
import functools
from typing import NamedTuple

import jax
import jax.numpy as jnp
import numpy as np
from jax import lax
from jax.experimental import pallas as pl
from jax.experimental.pallas import tpu as pltpu

N_META = 16
SB_HEAD_DIM = 64
SB_WIDTH = 512
CONV_WIDTH = 512
CONV_KERNEL = 31
N_GROUPS = 4
EXPERTS_PER_GROUP = 4
N_EXPERTS = N_GROUPS * EXPERTS_PER_GROUP
EXPERT_FF = 512
GROUP_PAIRS = [(a, b) for a in range(EXPERTS_PER_GROUP) for b in range(a + 1, EXPERTS_PER_GROUP)]
PAIRS_PER_GROUP = len(GROUP_PAIRS)
N_CLASSES = N_GROUPS * PAIRS_PER_GROUP
CLASS_ROWS = 32
EPS = 1e-6

LANES = 128
SUBLANES = 8
HEAD_PAIR = 2 * SB_HEAD_DIM
CONV_HALO = 32
ROUTER_ROWS = 32
STAY_SUM_EXIT = 150.0
LOG2_E = 1.4426950408889634
VMEM_LIMIT = 56 * 1024 * 1024

f32 = jnp.float32
bf16 = jnp.bfloat16


def _sigmoid(x):
    return 1.0 / (1.0 + jnp.exp(-x))


def _rms(x, g):
    return x * lax.rsqrt(jnp.mean(x * x, axis=-1, keepdims=True) + EPS) * g


def _const_spec(shape):
    return pl.BlockSpec(shape, lambda *_: (0,) * len(shape))


def _in_proj_kernel(x_ref, g_ref, w_ref, q_ref, k_ref, v_ref, u_ref, gs_ref, gc_ref):
    xb = _rms(x_ref[...], g_ref[...]).astype(bf16)

    def proj(lo, width):
        return jnp.dot(xb, w_ref[:, lo:lo + width], preferred_element_type=f32)

    o = 0
    q_ref[...] = (proj(o, SB_WIDTH) * (SB_HEAD_DIM ** -0.5 * LOG2_E)).astype(bf16); o += SB_WIDTH
    k_ref[...] = proj(o, SB_WIDTH).astype(bf16); o += SB_WIDTH
    v_ref[...] = proj(o, SB_WIDTH).astype(bf16); o += SB_WIDTH
    a = proj(o, CONV_WIDTH); o += CONV_WIDTH
    b = proj(o, CONV_WIDTH); o += CONV_WIDTH
    u_ref[...] = (a * _sigmoid(b)).astype(bf16)
    d = gs_ref.shape[1]
    gs_ref[...] = _sigmoid(proj(o, d)).astype(bf16); o += d
    gc_ref[...] = _sigmoid(proj(o, d)).astype(bf16)


def _in_proj(x2, g, w_b, tm):
    t, d = x2.shape
    n_in = w_b.shape[1]
    row = lambda w: pl.BlockSpec((tm, w), lambda i: (i, 0))
    sds = lambda w: jax.ShapeDtypeStruct((t, w), bf16)
    return pl.pallas_call(
        _in_proj_kernel,
        grid=(t // tm,),
        in_specs=[row(d), _const_spec((1, d)), _const_spec((d, n_in))],
        out_specs=[row(SB_WIDTH), row(SB_WIDTH), row(SB_WIDTH), row(CONV_WIDTH), row(d), row(d)],
        out_shape=[sds(SB_WIDTH), sds(SB_WIDTH), sds(SB_WIDTH), sds(CONV_WIDTH), sds(d), sds(d)],
        compiler_params=pltpu.CompilerParams(
            dimension_semantics=("arbitrary",), vmem_limit_bytes=VMEM_LIMIT),
        name="in_proj",
    )(x2, g, w_b)


def _attn_kernel(q_ref, k_ref, v_ref, km_ref, vm_ref, tri_ref, o_ref, acc_ref, c_ref, *, tq, n_pairs):
    i = pl.program_id(1)
    lane = lax.broadcasted_iota(jnp.int32, (1, LANES), 1)
    low = lane < SB_HEAD_DIM

    def visit(load_kv, valid):
        for p in range(n_pairs):
            cols = slice(p * HEAD_PAIR, (p + 1) * HEAD_PAIR)
            kt, vt = load_kv(cols)
            tk = kt.shape[0]
            tri = tri_ref[:tk, :tk]
            q2 = q_ref[0, :, cols]
            zero = jnp.zeros_like(q2)
            qq = jnp.concatenate([jnp.where(low, q2, zero), jnp.where(low, zero, q2)], axis=0)
            z = lax.dot_general(qq, kt, (((1,), (1,)), ((), ())),
                                preferred_element_type=f32)
            sp = jnp.maximum(z, 0.0) + jnp.log2(1.0 + jnp.exp2(-jnp.abs(z)))
            if valid is not None:
                sp = jnp.where(valid, sp, 0.0)
            spb = sp.astype(bf16)
            c = c_ref[p]
            after = jnp.tile(c, (1, tk // LANES)) + jnp.dot(spb, tri, preferred_element_type=f32)
            a = jnp.exp2((z - sp) - after)
            if valid is not None:
                a = jnp.where(valid, a, 0.0)
            pv = jnp.dot(a.astype(bf16), vt, preferred_element_type=f32)
            c_new = after[:, 0:1] + spb[:, 0:1].astype(f32)
            c_ref[p] = jnp.broadcast_to(c_new, c.shape)
            acc_ref[:, cols] += jnp.where(low, pv[:tq], pv[tq:])

    def seq_tile(j):
        off = pl.multiple_of(j * tq, tq)
        return lambda cols: (k_ref[0, pl.ds(off, tq), cols], v_ref[0, pl.ds(off, tq), cols])

    def more_needed():
        return jnp.min(c_ref[...]) < STAY_SUM_EXIT

    acc_ref[...] = jnp.zeros_like(acc_ref)
    c_ref[...] = jnp.zeros_like(c_ref)

    qrow = lax.broadcasted_iota(jnp.int32, (2 * tq, tq), 0) & (tq - 1)
    kcol = lax.broadcasted_iota(jnp.int32, (2 * tq, tq), 1)
    visit(seq_tile(i), kcol < qrow)

    @pl.when(i > 0)
    def _():
        visit(seq_tile(i - 1), None)

    def cond(carry):
        j, more = carry
        return jnp.logical_and(j >= 0, more)

    def body(carry):
        j, _ = carry
        visit(seq_tile(j), None)
        return j - 1, more_needed()

    _, more = lax.while_loop(cond, body, (i - 2, more_needed()))

    @pl.when(more)
    def _():
        mcol = lax.broadcasted_iota(jnp.int32, (2 * tq, LANES), 1)
        visit(lambda cols: (km_ref[:, cols], vm_ref[:, cols]), mcol < N_META)

    o_ref[0] = acc_ref[...].astype(o_ref.dtype)


def _attention(q, k, v, km, vm, tq):
    b, s, w = q.shape
    n_pairs = w // HEAD_PAIR
    tri = jnp.asarray(np.tril(np.ones((tq, tq), np.float32), -1), bf16)
    seq = pl.BlockSpec((1, s, w), lambda bi, i: (bi, 0, 0))
    tile = pl.BlockSpec((1, tq, w), lambda bi, i: (bi, i, 0))
    return pl.pallas_call(
        functools.partial(_attn_kernel, tq=tq, n_pairs=n_pairs),
        grid=(b, s // tq),
        in_specs=[tile, seq, seq, _const_spec((LANES, w)), _const_spec((LANES, w)),
                  _const_spec((tq, tq))],
        out_specs=tile,
        out_shape=jax.ShapeDtypeStruct((b, s, w), bf16),
        scratch_shapes=[pltpu.VMEM((tq, w), f32), pltpu.VMEM((n_pairs, 2 * tq, LANES), f32)],
        compiler_params=pltpu.CompilerParams(
            dimension_semantics=("arbitrary", "arbitrary"), vmem_limit_bytes=VMEM_LIMIT),
        name="sb_attention",
    )(q, k, v, km, vm, tri)


def _mix_kernel(att_ref, u_ref, prev_ref, mhalo_ref, gs_ref, gc_ref, x_ref,
                wsb_ref, wcv_ref, wout_ref, cw_ref, cb_ref, lng_ref, lnb_ref, nf_ref,
                wrt_ref, brt_ref, upper_ref,
                h_ref, xn_ref, route_ref, cnt_out_ref,
                ubuf_ref, shift_ref, ycv_ref, cnt_ref, *, tm, tiles_per_seq, chunk):
    i = pl.program_id(0)
    first = (i % tiles_per_seq) == 0
    halo = jnp.where(first, mhalo_ref[...], prev_ref[...])
    ubuf_ref[0:CONV_HALO, :] = halo.astype(f32)
    ubuf_ref[CONV_HALO:, :] = u_ref[...].astype(f32)

    lead = CONV_HALO - (CONV_KERNEL - 1)
    span = shift_ref.shape[1]
    for ph in range(1, SUBLANES):
        shift_ref[ph - 1] = ubuf_ref[ph:ph + span, :]

    def tap(r0, kk):
        ph = (lead + kk) % SUBLANES
        base = r0 + lead + kk - ph
        if ph == 0:
            return ubuf_ref[base:base + chunk, :]
        return shift_ref[ph - 1, base:base + chunk, :]

    for r0 in range(0, tm, chunk):
        y = jnp.broadcast_to(cb_ref[...], (chunk, CONV_WIDTH))
        for kk in range(CONV_KERNEL):
            y = y + cw_ref[kk:kk + 1, :] * tap(r0, kk)
        mu = jnp.mean(y, axis=-1, keepdims=True)
        yc = y - mu
        var = jnp.mean(yc * yc, axis=-1, keepdims=True)
        yn = yc * lax.rsqrt(var + EPS) * lng_ref[...] + lnb_ref[...]
        ycv_ref[r0:r0 + chunk, :] = (yn * _sigmoid(yn)).astype(bf16)

    y_sb = jnp.dot(att_ref[...], wsb_ref[...], preferred_element_type=f32)
    y_cv = jnp.dot(ycv_ref[...], wcv_ref[...], preferred_element_type=f32)
    merged = gs_ref[...].astype(f32) * y_sb + gc_ref[...].astype(f32) * y_cv
    h = x_ref[...] + jnp.dot(merged.astype(bf16), wout_ref[...], preferred_element_type=f32)
    h_ref[...] = h
    xn = _rms(h, nf_ref[...])
    xnb = xn.astype(bf16)
    d_model = xn.shape[1]
    xn_ref[:, :d_model] = xn

    lt = lax.dot_general(wrt_ref[...], xnb, (((1,), (1,)), ((), ())),
                         preferred_element_type=f32) + brt_ref[...]
    gl = [lt[N_EXPERTS + g:N_EXPERTS + g + 1, :] for g in range(N_GROUPS)]
    gmax = functools.reduce(jnp.maximum, gl)
    gsum = functools.reduce(lambda a, b: a + b, [jnp.exp(x - gmax) for x in gl])
    g_p = 1.0 / gsum
    gidx = jnp.full_like(gmax, N_GROUPS - 1)
    for g in range(N_GROUPS - 2, -1, -1):
        gidx = jnp.where(gl[g] == gmax, float(g), gidx)
    el = []
    for j in range(EXPERTS_PER_GROUP):
        x = lt[(N_GROUPS - 1) * EXPERTS_PER_GROUP + j:(N_GROUPS - 1) * EXPERTS_PER_GROUP + j + 1, :]
        for g in range(N_GROUPS - 2, -1, -1):
            x = jnp.where(gidx == float(g), lt[g * EXPERTS_PER_GROUP + j:g * EXPERTS_PER_GROUP + j + 1, :], x)
        el.append(x)
    m1 = functools.reduce(jnp.maximum, el)
    i1 = jnp.full_like(m1, EXPERTS_PER_GROUP - 1)
    for j in range(EXPERTS_PER_GROUP - 2, -1, -1):
        i1 = jnp.where(el[j] == m1, float(j), i1)
    el2 = [jnp.where(i1 == float(j), -jnp.inf, el[j]) for j in range(EXPERTS_PER_GROUP)]
    m2 = functools.reduce(jnp.maximum, el2)
    i2 = jnp.full_like(m2, EXPERTS_PER_GROUP - 1)
    for j in range(EXPERTS_PER_GROUP - 2, -1, -1):
        i2 = jnp.where(el2[j] == m2, float(j), i2)
    w1 = g_p / (1.0 + jnp.exp(m2 - m1))
    w2 = g_p - w1
    lo = jnp.minimum(i1, i2)
    hi = jnp.maximum(i1, i2)
    pair = jnp.where(lo == 0.0, hi - 1.0, jnp.where(lo == 1.0, hi + 1.0, float(PAIRS_PER_GROUP - 1)))
    cls = gidx * PAIRS_PER_GROUP + pair
    first_is_lo = i1 < i2
    w_lo = jnp.where(first_is_lo, w1, w2)
    w_hi = jnp.where(first_is_lo, w2, w1)

    @pl.when(i == 0)
    def _():
        cnt_ref[...] = jnp.zeros_like(cnt_ref)

    in_cls = lax.broadcasted_iota(jnp.int32, (CLASS_ROWS, tm), 0).astype(f32) == cls
    member = jnp.where(in_cls, 1.0, 0.0).astype(bf16)
    before = jnp.dot(member, upper_ref[...], preferred_element_type=f32)
    cnt = cnt_ref[...]
    rank = jnp.sum(jnp.where(in_cls, before + jnp.tile(cnt, (1, tm // LANES)), 0.0),
                   axis=0, keepdims=True)
    cnt_ref[...] = cnt + jnp.dot(member, jnp.ones((tm, LANES), bf16), preferred_element_type=f32)
    cnt_out_ref[...] = cnt_ref[...]
    pad = jnp.zeros((6, tm), f32)
    route_ref[...] = jnp.concatenate([cls, rank, pad], axis=0).astype(jnp.int32)
    wrow = lax.broadcasted_iota(jnp.int32, (LANES, tm), 0)
    w_t = jnp.where(wrow == 0, w_lo, jnp.where(wrow == 1, w_hi, 0.0))
    xn_ref[:, d_model:] = w_t.T


def _mix(att, u, mhalo, gs, gc, x2, wsb, wcv, wout, cw, cb, lng, lnb, nf, wrt, brt, tm, seq_len):
    t, d = x2.shape
    tiles_per_seq = seq_len // tm
    row = lambda w: pl.BlockSpec((tm, w), lambda i: (i, 0))
    halo_blocks = tm // CONV_HALO
    prev = pl.BlockSpec((CONV_HALO, CONV_WIDTH), lambda i: (jnp.maximum(i * halo_blocks - 1, 0), 0))
    kern = functools.partial(_mix_kernel, tm=tm, tiles_per_seq=tiles_per_seq, chunk=64)
    upper = jnp.asarray(np.triu(np.ones((tm, tm), np.float32), 1), bf16)
    return pl.pallas_call(
        kern,
        grid=(t // tm,),
        in_specs=[row(SB_WIDTH), row(CONV_WIDTH), prev, _const_spec((CONV_HALO, CONV_WIDTH)),
                  row(d), row(d), row(d),
                  _const_spec((SB_WIDTH, d)), _const_spec((CONV_WIDTH, d)), _const_spec((d, d)),
                  _const_spec((CONV_KERNEL, CONV_WIDTH)), _const_spec((1, CONV_WIDTH)),
                  _const_spec((1, CONV_WIDTH)), _const_spec((1, CONV_WIDTH)), _const_spec((1, d)),
                  _const_spec((ROUTER_ROWS, d)), _const_spec((ROUTER_ROWS, 1)), _const_spec((tm, tm))],
        out_specs=[row(d), row(d + LANES), pl.BlockSpec((8, tm), lambda i: (0, i)),
                   _const_spec((CLASS_ROWS, LANES))],
        out_shape=[jax.ShapeDtypeStruct((t, d), f32), jax.ShapeDtypeStruct((t, d + LANES), f32),
                   jax.ShapeDtypeStruct((8, t), jnp.int32),
                   jax.ShapeDtypeStruct((CLASS_ROWS, LANES), f32)],
        scratch_shapes=[pltpu.VMEM((tm + CONV_HALO, CONV_WIDTH), f32),
                        pltpu.VMEM((SUBLANES - 1, tm + CONV_HALO - SUBLANES, CONV_WIDTH), f32),
                        pltpu.VMEM((tm, CONV_WIDTH), bf16),
                        pltpu.VMEM((CLASS_ROWS, LANES), f32)],
        compiler_params=pltpu.CompilerParams(
            dimension_semantics=("arbitrary",), vmem_limit_bytes=VMEM_LIMIT),
        name="mix",
    )(att, u, u, mhalo, gs, gc, x2, wsb, wcv, wout, cw, cb, lng, lnb, nf, wrt, brt, upper)


def _row_copy(src, src_row, dst, dst_row, sem):
    return pltpu.make_async_copy(src.at[pl.ds(src_row, 1)], dst.at[pl.ds(dst_row, 1)], sem)


def _scatter_kernel(zt_ref, pos_ref, x_ref, xs_hbm, xbuf, zbuf, sems, zsem, *, tm, te):
    i = pl.program_id(0)
    slot = i % 2

    @pl.when(i == 0)
    def _():
        zbuf[...] = jnp.zeros_like(zbuf)

        def zero_tile(k):
            start = pl.multiple_of(zt_ref[k] * te, te)
            return pltpu.make_async_copy(zbuf, xs_hbm.at[pl.ds(start, te)], zsem)

        for k in range(zt_ref.shape[0]):
            @pl.when(zt_ref[k] >= 0)
            def _():
                zero_tile(k).start()
        for k in range(zt_ref.shape[0]):
            @pl.when(zt_ref[k] >= 0)
            def _():
                zero_tile(k).wait()

    xbuf[slot] = x_ref[...]
    src = xbuf.at[slot]
    for r in range(tm):
        _row_copy(src, r, xs_hbm, pos_ref[0, 0, r], sems.at[slot]).start(priority=r % 2)

    def wait_tile(sl):
        pltpu.make_async_copy(xbuf.at[sl], xs_hbm.at[pl.ds(0, tm)], sems.at[sl]).wait()

    @pl.when(i > 0)
    def _():
        wait_tile(1 - slot)

    @pl.when(i == pl.num_programs(0) - 1)
    def _():
        wait_tile(slot)


def _scatter(zero_tiles, pos3, rows, n_rows, te):
    n_tiles, _, tm = pos3.shape
    w = rows.shape[1]
    grid_spec = pltpu.PrefetchScalarGridSpec(
        num_scalar_prefetch=1,
        grid=(n_tiles,),
        in_specs=[pl.BlockSpec((1, 1, tm), lambda i, zt: (i, 0, 0), memory_space=pltpu.SMEM),
                  pl.BlockSpec((tm, w), lambda i, zt: (i, 0))],
        out_specs=pl.BlockSpec(memory_space=pl.ANY),
        scratch_shapes=[pltpu.VMEM((2, tm, w), rows.dtype), pltpu.VMEM((te, w), rows.dtype),
                        pltpu.SemaphoreType.DMA((2,)), pltpu.SemaphoreType.DMA(())],
    )
    return pl.pallas_call(
        functools.partial(_scatter_kernel, tm=tm, te=te),
        grid_spec=grid_spec,
        out_shape=jax.ShapeDtypeStruct((n_rows, w), rows.dtype),
        compiler_params=pltpu.CompilerParams(
            dimension_semantics=("arbitrary",), vmem_limit_bytes=VMEM_LIMIT),
        name="moe_scatter",
    )(zero_tiles, pos3, rows)


def _experts_kernel(lo_ref, hi_ref, nu_ref, xs_ref, wgu_lo_ref, wd_lo_ref, wgu_hi_ref, wd_hi_ref,
                    ys_ref):
    del lo_ref, hi_ref
    i = pl.program_id(0)
    d = ys_ref.shape[1]

    @pl.when(i < nu_ref[0])
    def _():
        x = xs_ref[:, :d].astype(bf16)
        wts = xs_ref[:, d:]

        def scaled_hidden(wgu_ref, col):
            gu = jnp.dot(x, wgu_ref[0], preferred_element_type=f32)
            g = gu[:, :EXPERT_FF]
            return ((g * _sigmoid(g)) * gu[:, EXPERT_FF:] * wts[:, col:col + 1]).astype(bf16)

        ys_ref[...] = (
            jnp.dot(scaled_hidden(wgu_lo_ref, 0), wd_lo_ref[0], preferred_element_type=f32)
            + jnp.dot(scaled_hidden(wgu_hi_ref, 1), wd_hi_ref[0], preferred_element_type=f32))

    @pl.when(i >= nu_ref[0])
    def _():
        ys_ref[...] = jnp.zeros_like(ys_ref)


def _experts(tile_lo, tile_hi, n_used, xs, wgu, wd, tm):
    n_rows, w = xs.shape
    d = wd.shape[2]
    up = lambda which: pl.BlockSpec((1, d, 2 * EXPERT_FF), lambda i, lo, hi, nu: ((lo, hi)[which][i], 0, 0))
    down = lambda which: pl.BlockSpec((1, EXPERT_FF, d), lambda i, lo, hi, nu: ((lo, hi)[which][i], 0, 0))
    grid_spec = pltpu.PrefetchScalarGridSpec(
        num_scalar_prefetch=3,
        grid=(n_rows // tm,),
        in_specs=[pl.BlockSpec((tm, w), lambda i, lo, hi, nu: (i, 0)), up(0), down(0), up(1), down(1)],
        out_specs=pl.BlockSpec((tm, d), lambda i, lo, hi, nu: (i, 0)),
    )
    return pl.pallas_call(
        _experts_kernel,
        grid_spec=grid_spec,
        out_shape=jax.ShapeDtypeStruct((n_rows, d), f32),
        compiler_params=pltpu.CompilerParams(
            dimension_semantics=("arbitrary",), vmem_limit_bytes=VMEM_LIMIT),
        name="moe_experts",
    )(tile_lo, tile_hi, n_used, xs, wgu, wd, wgu, wd)


def _combine_kernel(pos_ref, nxt_ref, ys_hbm, h_ref, nf_ref, o_ref, gbuf, sems, *, tm):
    i = pl.program_id(0)
    slot = i % 2

    def issue(p_ref, sl):
        for r in range(tm):
            _row_copy(ys_hbm, p_ref[0, 0, r], gbuf.at[sl], r, sems.at[sl]).start(priority=r % 2)

    @pl.when(i == 0)
    def _():
        issue(pos_ref, 0)

    @pl.when(i + 1 < pl.num_programs(0))
    def _():
        issue(nxt_ref, 1 - slot)

    pltpu.make_async_copy(ys_hbm.at[pl.ds(0, tm)], gbuf.at[slot], sems.at[slot]).wait()
    o_ref[...] = _rms(h_ref[...] + gbuf[slot], nf_ref[...])


def _combine(pos3, ys, h, nf):
    n_tiles, _, tm = pos3.shape
    t, d = h.shape
    row = pl.BlockSpec((tm, d), lambda i: (i, 0))
    last = n_tiles - 1
    return pl.pallas_call(
        functools.partial(_combine_kernel, tm=tm),
        grid=(n_tiles,),
        in_specs=[pl.BlockSpec((1, 1, tm), lambda i: (i, 0, 0), memory_space=pltpu.SMEM),
                  pl.BlockSpec((1, 1, tm), lambda i: (jnp.minimum(i + 1, last), 0, 0),
                               memory_space=pltpu.SMEM),
                  pl.BlockSpec(memory_space=pl.ANY), row, _const_spec((1, d))],
        out_specs=row,
        out_shape=jax.ShapeDtypeStruct((t, d), f32),
        scratch_shapes=[pltpu.VMEM((2, tm, d), f32), pltpu.SemaphoreType.DMA((2,))],
        compiler_params=pltpu.CompilerParams(
            dimension_semantics=("arbitrary",), vmem_limit_bytes=VMEM_LIMIT),
        name="moe_combine",
    )(pos3, pos3, ys, h, nf)


def _tile_positions(pos, tm):
    return pos.reshape(-1, 1, tm)


class _Tiles(NamedTuple):
    proj: int
    rows: int
    query: int
    expert: int
    scatter: int
    combine: int

    @classmethod
    def plan(cls, seq_len):
        cap = lambda n: min(n, seq_len)
        return cls(proj=cap(1024), rows=cap(512), query=cap(256), expert=cap(512),
                   scatter=cap(2048), combine=cap(1024))


def kernel(x, meta, norm_mix, w_in, w_sb_o, conv_w, conv_b, conv_ln_g, conv_ln_b, w_conv_o, w_out,
           norm_ffn, w_router_group, b_router_group, w_router_expert, b_router_expert, w_gate, w_up,
           w_down, norm_final):
    assert norm_mix.shape[0] == 1, "single-layer block"
    b, s, d = x.shape
    t = b * s
    tiles_of = _Tiles.plan(s)
    tm, tq, te = tiles_of.rows, tiles_of.query, tiles_of.expert
    x2 = x.reshape(t, d)
    w_in_b = w_in[0].astype(bf16)
    g_mix = norm_mix[0][None]

    q, k, v, u, gs, gc = _in_proj(x2, g_mix, w_in_b, tiles_of.proj)
    _, km, vm, um, _, _ = _in_proj(meta, g_mix, w_in_b, N_META)
    pad_rows = lambda a, top, bottom: jnp.pad(a, ((top, bottom), (0, 0)))
    km = pad_rows(km, 0, LANES - N_META)
    vm = pad_rows(vm, 0, LANES - N_META)
    shp = (b, s, SB_WIDTH)
    att = _attention(q.reshape(shp), k.reshape(shp), v.reshape(shp), km, vm, tq).reshape(t, SB_WIDTH)

    mhalo = pad_rows(um, CONV_HALO - N_META, 0)
    wrt = jnp.concatenate([w_router_expert[0], w_router_group[0]], axis=1).T
    wrt = pad_rows(wrt, 0, ROUTER_ROWS - wrt.shape[0]).astype(bf16)
    brt = jnp.concatenate([b_router_expert[0], b_router_group[0]])[:, None]
    brt = pad_rows(brt, 0, ROUTER_ROWS - brt.shape[0]).astype(f32)
    h, xrow, route, counts = _mix(
        att, u, mhalo, gs, gc, x2,
        w_sb_o[0].astype(bf16), w_conv_o[0].astype(bf16), w_out[0].astype(bf16),
        conv_w[0], conv_b[0][None], conv_ln_g[0][None], conv_ln_b[0][None],
        norm_ffn[0][None], wrt, brt, tm, s)

    n_rows = t + N_CLASSES * te
    cnt = counts[:N_CLASSES, 0].astype(jnp.int32)
    tiles = (cnt + te - 1) // te
    tile_end = jnp.cumsum(tiles)
    row_start = (tile_end - tiles) * te
    classes = jnp.arange(N_CLASSES, dtype=jnp.int32)
    pos = jnp.sum(jnp.where(route[0][:, None] == classes, row_start, 0), axis=-1) + route[1]
    tile_ids = jnp.arange(n_rows // te, dtype=jnp.int32)
    tile_cls = jnp.minimum(jnp.sum(tile_ids[:, None] >= tile_end, axis=-1), N_CLASSES - 1)
    cls_hot = tile_cls[:, None] == classes
    lo_of = jnp.array([c // PAIRS_PER_GROUP * EXPERTS_PER_GROUP + GROUP_PAIRS[c % PAIRS_PER_GROUP][0]
                       for c in range(N_CLASSES)], jnp.int32)
    hi_of = jnp.array([c // PAIRS_PER_GROUP * EXPERTS_PER_GROUP + GROUP_PAIRS[c % PAIRS_PER_GROUP][1]
                       for c in range(N_CLASSES)], jnp.int32)
    tile_lo = jnp.sum(jnp.where(cls_hot, lo_of, 0), axis=-1).astype(jnp.int32)
    tile_hi = jnp.sum(jnp.where(cls_hot, hi_of, 0), axis=-1).astype(jnp.int32)
    n_used = tile_end[-1:].astype(jnp.int32)
    last_tile = jnp.where(tiles > 0, tile_end - 1, -1)
    tail = n_used + classes
    zero_tiles = jnp.concatenate([last_tile, jnp.where(tail < n_rows // te, tail, -1)]).astype(jnp.int32)

    xs = _scatter(zero_tiles, _tile_positions(pos, tiles_of.scatter), xrow, n_rows, te)
    wgu = jnp.concatenate([w_gate[0], w_up[0]], axis=-1).astype(bf16)
    ys = _experts(tile_lo, tile_hi, n_used, xs, wgu, w_down[0].astype(bf16), te)
    out = _combine(_tile_positions(pos, tiles_of.combine), ys, h, norm_final[None])
    return out.reshape(b, s, d)
```

```python
import functools
from typing import NamedTuple

import jax
import jax.numpy as jnp
import numpy as np
from jax import lax
from jax.experimental import pallas as pl
from jax.experimental.pallas import tpu as pltpu

N_META = 16
SB_HEAD_DIM = 64
SB_WIDTH = 512
CONV_WIDTH = 512
CONV_KERNEL = 31
N_GROUPS = 4
EXPERTS_PER_GROUP = 4
N_EXPERTS = N_GROUPS * EXPERTS_PER_GROUP
EXPERT_FF = 512
GROUP_PAIRS = [(a, b) for a in range(EXPERTS_PER_GROUP) for b in range(a + 1, EXPERTS_PER_GROUP)]
PAIRS_PER_GROUP = len(GROUP_PAIRS)
N_CLASSES = N_GROUPS * PAIRS_PER_GROUP
CLASS_ROWS = 32
EPS = 1e-6

LANES = 128
SUBLANES = 8
HEAD_PAIR = 2 * SB_HEAD_DIM
CONV_HALO = 32
ROUTER_ROWS = 32
STAY_SUM_EXIT = 150.0
LOG2_E = 1.4426950408889634
VMEM_LIMIT = 56 * 1024 * 1024

f32 = jnp.float32
bf16 = jnp.bfloat16


def _sigmoid(x):
    return 1.0 / (1.0 + jnp.exp(-x))


def _rms(x, g):
    return x * lax.rsqrt(jnp.mean(x * x, axis=-1, keepdims=True) + EPS) * g


def _const_spec(shape):
    return pl.BlockSpec(shape, lambda *_: (0,) * len(shape))


def _in_proj_kernel(x_ref, g_ref, w_ref, q_ref, k_ref, v_ref, u_ref, gs_ref, gc_ref):
    xb = _rms(x_ref[...], g_ref[...]).astype(bf16)

    def proj(lo, width):
        return jnp.dot(xb, w_ref[:, lo:lo + width], preferred_element_type=f32)

    o = 0
    q_ref[...] = (proj(o, SB_WIDTH) * (SB_HEAD_DIM ** -0.5 * LOG2_E)).astype(bf16); o += SB_WIDTH
    k_ref[...] = proj(o, SB_WIDTH).astype(bf16); o += SB_WIDTH
    v_ref[...] = proj(o, SB_WIDTH).astype(bf16); o += SB_WIDTH
    a = proj(o, CONV_WIDTH); o += CONV_WIDTH
    b = proj(o, CONV_WIDTH); o += CONV_WIDTH
    u_ref[...] = (a * _sigmoid(b)).astype(bf16)
    d = gs_ref.shape[1]
    gs_ref[...] = _sigmoid(proj(o, d)).astype(bf16); o += d
    gc_ref[...] = _sigmoid(proj(o, d)).astype(bf16)


def _in_proj(x2, g, w_b, tm):
    t, d = x2.shape
    n_in = w_b.shape[1]
    row = lambda w: pl.BlockSpec((tm, w), lambda i: (i, 0))
    sds = lambda w: jax.ShapeDtypeStruct((t, w), bf16)
    return pl.pallas_call(
        _in_proj_kernel,
        grid=(t // tm,),
        in_specs=[row(d), _const_spec((1, d)), _const_spec((d, n_in))],
        out_specs=[row(SB_WIDTH), row(SB_WIDTH), row(SB_WIDTH), row(CONV_WIDTH), row(d), row(d)],
        out_shape=[sds(SB_WIDTH), sds(SB_WIDTH), sds(SB_WIDTH), sds(CONV_WIDTH), sds(d), sds(d)],
        compiler_params=pltpu.CompilerParams(
            dimension_semantics=("arbitrary",), vmem_limit_bytes=VMEM_LIMIT),
        name="in_proj",
    )(x2, g, w_b)


def _attn_kernel(q_ref, k_ref, v_ref, km_ref, vm_ref, tri_ref, o_ref, acc_ref, c_ref, *, tq, n_pairs):
    i = pl.program_id(1)
    lane = lax.broadcasted_iota(jnp.int32, (1, LANES), 1)
    low = lane < SB_HEAD_DIM

    def visit(load_kv, valid):
        for p in range(n_pairs):
            cols = slice(p * HEAD_PAIR, (p + 1) * HEAD_PAIR)
            kt, vt = load_kv(cols)
            tk = kt.shape[0]
            tri = tri_ref[:tk, :tk]
            q2 = q_ref[0, :, cols]
            zero = jnp.zeros_like(q2)
            qq = jnp.concatenate([jnp.where(low, q2, zero), jnp.where(low, zero, q2)], axis=0)
            z = lax.dot_general(qq, kt, (((1,), (1,)), ((), ())),
                                preferred_element_type=f32)
            sp = jnp.maximum(z, 0.0) + jnp.log2(1.0 + jnp.exp2(-jnp.abs(z)))
            if valid is not None:
                sp = jnp.where(valid, sp, 0.0)
            spb = sp.astype(bf16)
            c = c_ref[p]
            after = jnp.tile(c, (1, tk // LANES)) + jnp.dot(spb, tri, preferred_element_type=f32)
            a = jnp.exp2((z - sp) - after)
            if valid is not None:
                a = jnp.where(valid, a, 0.0)
            pv = jnp.dot(a.astype(bf16), vt, preferred_element_type=f32)
            c_new = after[:, 0:1] + spb[:, 0:1].astype(f32)
            c_ref[p] = jnp.broadcast_to(c_new, c.shape)
            acc_ref[:, cols] += jnp.where(low, pv[:tq], pv[tq:])

    def seq_tile(j):
        off = pl.multiple_of(j * tq, tq)
        return lambda cols: (k_ref[0, pl.ds(off, tq), cols], v_ref[0, pl.ds(off, tq), cols])

    def more_needed():
        return jnp.min(c_ref[...]) < STAY_SUM_EXIT

    acc_ref[...] = jnp.zeros_like(acc_ref)
    c_ref[...] = jnp.zeros_like(c_ref)

    qrow = lax.broadcasted_iota(jnp.int32, (2 * tq, tq), 0) & (tq - 1)
    kcol = lax.broadcasted_iota(jnp.int32, (2 * tq, tq), 1)
    visit(seq_tile(i), kcol < qrow)

    @pl.when(i > 0)
    def _():
        visit(seq_tile(i - 1), None)

    def cond(carry):
        j, more = carry
        return jnp.logical_and(j >= 0, more)

    def body(carry):
        j, _ = carry
        visit(seq_tile(j), None)
        return j - 1, more_needed()

    _, more = lax.while_loop(cond, body, (i - 2, more_needed()))

    @pl.when(more)
    def _():
        mcol = lax.broadcasted_iota(jnp.int32, (2 * tq, LANES), 1)
        visit(lambda cols: (km_ref[:, cols], vm_ref[:, cols]), mcol < N_META)

    o_ref[0] = acc_ref[...].astype(o_ref.dtype)


def _attention(q, k, v, km, vm, tq):
    b, s, w = q.shape
    n_pairs = w // HEAD_PAIR
    tri = jnp.asarray(np.tril(np.ones((tq, tq), np.float32), -1), bf16)
    seq = pl.BlockSpec((1, s, w), lambda bi, i: (bi, 0, 0))
    tile = pl.BlockSpec((1, tq, w), lambda bi, i: (bi, i, 0))
    return pl.pallas_call(
        functools.partial(_attn_kernel, tq=tq, n_pairs=n_pairs),
        grid=(b, s // tq),
        in_specs=[tile, seq, seq, _const_spec((LANES, w)), _const_spec((LANES, w)),
                  _const_spec((tq, tq))],
        out_specs=tile,
        out_shape=jax.ShapeDtypeStruct((b, s, w), bf16),
        scratch_shapes=[pltpu.VMEM((tq, w), f32), pltpu.VMEM((n_pairs, 2 * tq, LANES), f32)],
        compiler_params=pltpu.CompilerParams(
            dimension_semantics=("arbitrary", "arbitrary"), vmem_limit_bytes=VMEM_LIMIT),
        name="sb_attention",
    )(q, k, v, km, vm, tri)


def _mix_kernel(att_ref, u_ref, prev_ref, mhalo_ref, gs_ref, gc_ref, x_ref,
                wsb_ref, wcv_ref, wout_ref, cw_ref, cb_ref, lng_ref, lnb_ref, nf_ref,
                wrt_ref, brt_ref, upper_ref,
                h_ref, xn_ref, route_ref, cnt_out_ref,
                ubuf_ref, shift_ref, ycv_ref, cnt_ref, *, tm, tiles_per_seq, chunk):
    i = pl.program_id(0)
    first = (i % tiles_per_seq) == 0
    halo = jnp.where(first, mhalo_ref[...], prev_ref[...])
    ubuf_ref[0:CONV_HALO, :] = halo.astype(f32)
    ubuf_ref[CONV_HALO:, :] = u_ref[...].astype(f32)

    lead = CONV_HALO - (CONV_KERNEL - 1)
    span = shift_ref.shape[1]
    for ph in range(1, SUBLANES):
        shift_ref[ph - 1] = ubuf_ref[ph:ph + span, :]

    def tap(r0, kk):
        ph = (lead + kk) % SUBLANES
        base = r0 + lead + kk - ph
        if ph == 0:
            return ubuf_ref[base:base + chunk, :]
        return shift_ref[ph - 1, base:base + chunk, :]

    for r0 in range(0, tm, chunk):
        y = jnp.broadcast_to(cb_ref[...], (chunk, CONV_WIDTH))
        for kk in range(CONV_KERNEL):
            y = y + cw_ref[kk:kk + 1, :] * tap(r0, kk)
        mu = jnp.mean(y, axis=-1, keepdims=True)
        yc = y - mu
        var = jnp.mean(yc * yc, axis=-1, keepdims=True)
        yn = yc * lax.rsqrt(var + EPS) * lng_ref[...] + lnb_ref[...]
        ycv_ref[r0:r0 + chunk, :] = (yn * _sigmoid(yn)).astype(bf16)

    y_sb = jnp.dot(att_ref[...], wsb_ref[...], preferred_element_type=f32)
    y_cv = jnp.dot(ycv_ref[...], wcv_ref[...], preferred_element_type=f32)
    merged = gs_ref[...].astype(f32) * y_sb + gc_ref[...].astype(f32) * y_cv
    h = x_ref[...] + jnp.dot(merged.astype(bf16), wout_ref[...], preferred_element_type=f32)
    h_ref[...] = h
    xn = _rms(h, nf_ref[...])
    xnb = xn.astype(bf16)
    d_model = xn.shape[1]
    xn_ref[:, :d_model] = xn

    lt = lax.dot_general(wrt_ref[...], xnb, (((1,), (1,)), ((), ())),
                         preferred_element_type=f32) + brt_ref[...]
    gl = [lt[N_EXPERTS + g:N_EXPERTS + g + 1, :] for g in range(N_GROUPS)]
    gmax = functools.reduce(jnp.maximum, gl)
    gsum = functools.reduce(lambda a, b: a + b, [jnp.exp(x - gmax) for x in gl])
    g_p = 1.0 / gsum
    gidx = jnp.full_like(gmax, N_GROUPS - 1)
    for g in range(N_GROUPS - 2, -1, -1):
        gidx = jnp.where(gl[g] == gmax, float(g), gidx)
    el = []
    for j in range(EXPERTS_PER_GROUP):
        x = lt[(N_GROUPS - 1) * EXPERTS_PER_GROUP + j:(N_GROUPS - 1) * EXPERTS_PER_GROUP + j + 1, :]
        for g in range(N_GROUPS - 2, -1, -1):
            x = jnp.where(gidx == float(g), lt[g * EXPERTS_PER_GROUP + j:g * EXPERTS_PER_GROUP + j + 1, :], x)
        el.append(x)
    m1 = functools.reduce(jnp.maximum, el)
    i1 = jnp.full_like(m1, EXPERTS_PER_GROUP - 1)
    for j in range(EXPERTS_PER_GROUP - 2, -1, -1):
        i1 = jnp.where(el[j] == m1, float(j), i1)
    el2 = [jnp.where(i1 == float(j), -jnp.inf, el[j]) for j in range(EXPERTS_PER_GROUP)]
    m2 = functools.reduce(jnp.maximum, el2)
    i2 = jnp.full_like(m2, EXPERTS_PER_GROUP - 1)
    for j in range(EXPERTS_PER_GROUP - 2, -1, -1):
        i2 = jnp.where(el2[j] == m2, float(j), i2)
    w1 = g_p / (1.0 + jnp.exp(m2 - m1))
    w2 = g_p - w1
    lo = jnp.minimum(i1, i2)
    hi = jnp.maximum(i1, i2)
    pair = jnp.where(lo == 0.0, hi - 1.0, jnp.where(lo == 1.0, hi + 1.0, float(PAIRS_PER_GROUP - 1)))
    cls = gidx * PAIRS_PER_GROUP + pair
    first_is_lo = i1 < i2
    w_lo = jnp.where(first_is_lo, w1, w2)
    w_hi = jnp.where(first_is_lo, w2, w1)

    @pl.when(i == 0)
    def _():
        cnt_ref[...] = jnp.zeros_like(cnt_ref)

    in_cls = lax.broadcasted_iota(jnp.int32, (CLASS_ROWS, tm), 0).astype(f32) == cls
    member = jnp.where(in_cls, 1.0, 0.0).astype(bf16)
    before = jnp.dot(member, upper_ref[...], preferred_element_type=f32)
    cnt = cnt_ref[...]
    rank = jnp.sum(jnp.where(in_cls, before + jnp.tile(cnt, (1, tm // LANES)), 0.0),
                   axis=0, keepdims=True)
    cnt_ref[...] = cnt + jnp.dot(member, jnp.ones((tm, LANES), bf16), preferred_element_type=f32)
    cnt_out_ref[...] = cnt_ref[...]
    pad = jnp.zeros((6, tm), f32)
    route_ref[...] = jnp.concatenate([cls, rank, pad], axis=0).astype(jnp.int32)
    wrow = lax.broadcasted_iota(jnp.int32, (LANES, tm), 0)
    w_t = jnp.where(wrow == 0, w_lo, jnp.where(wrow == 1, w_hi, 0.0))
    xn_ref[:, d_model:] = w_t.T


def _mix(att, u, mhalo, gs, gc, x2, wsb, wcv, wout, cw, cb, lng, lnb, nf, wrt, brt, tm, seq_len):
    t, d = x2.shape
    tiles_per_seq = seq_len // tm
    row = lambda w: pl.BlockSpec((tm, w), lambda i: (i, 0))
    halo_blocks = tm // CONV_HALO
    prev = pl.BlockSpec((CONV_HALO, CONV_WIDTH), lambda i: (jnp.maximum(i * halo_blocks - 1, 0), 0))
    kern = functools.partial(_mix_kernel, tm=tm, tiles_per_seq=tiles_per_seq, chunk=64)
    upper = jnp.asarray(np.triu(np.ones((tm, tm), np.float32), 1), bf16)
    return pl.pallas_call(
        kern,
        grid=(t // tm,),
        in_specs=[row(SB_WIDTH), row(CONV_WIDTH), prev, _const_spec((CONV_HALO, CONV_WIDTH)),
                  row(d), row(d), row(d),
                  _const_spec((SB_WIDTH, d)), _const_spec((CONV_WIDTH, d)), _const_spec((d, d)),
                  _const_spec((CONV_KERNEL, CONV_WIDTH)), _const_spec((1, CONV_WIDTH)),
                  _const_spec((1, CONV_WIDTH)), _const_spec((1, CONV_WIDTH)), _const_spec((1, d)),
                  _const_spec((ROUTER_ROWS, d)), _const_spec((ROUTER_ROWS, 1)), _const_spec((tm, tm))],
        out_specs=[row(d), row(d + LANES), pl.BlockSpec((8, tm), lambda i: (0, i)),
                   _const_spec((CLASS_ROWS, LANES))],
        out_shape=[jax.ShapeDtypeStruct((t, d), f32), jax.ShapeDtypeStruct((t, d + LANES), f32),
                   jax.ShapeDtypeStruct((8, t), jnp.int32),
                   jax.ShapeDtypeStruct((CLASS_ROWS, LANES), f32)],
        scratch_shapes=[pltpu.VMEM((tm + CONV_HALO, CONV_WIDTH), f32),
                        pltpu.VMEM((SUBLANES - 1, tm + CONV_HALO - SUBLANES, CONV_WIDTH), f32),
                        pltpu.VMEM((tm, CONV_WIDTH), bf16),
                        pltpu.VMEM((CLASS_ROWS, LANES), f32)],
        compiler_params=pltpu.CompilerParams(
            dimension_semantics=("arbitrary",), vmem_limit_bytes=VMEM_LIMIT),
        name="mix",
    )(att, u, u, mhalo, gs, gc, x2, wsb, wcv, wout, cw, cb, lng, lnb, nf, wrt, brt, upper)


def _row_copy(src, src_row, dst, dst_row, sem):
    return pltpu.make_async_copy(src.at[pl.ds(src_row, 1)], dst.at[pl.ds(dst_row, 1)], sem)


def _scatter_kernel(zt_ref, pos_ref, x_ref, xs_hbm, xbuf, zbuf, sems, zsem, *, tm, te):
    i = pl.program_id(0)
    slot = i % 2

    @pl.when(i == 0)
    def _():
        zbuf[...] = jnp.zeros_like(zbuf)

        def zero_tile(k):
            start = pl.multiple_of(zt_ref[k] * te, te)
            return pltpu.make_async_copy(zbuf, xs_hbm.at[pl.ds(start, te)], zsem)

        for k in range(zt_ref.shape[0]):
            @pl.when(zt_ref[k] >= 0)
            def _():
                zero_tile(k).start()
        for k in range(zt_ref.shape[0]):
            @pl.when(zt_ref[k] >= 0)
            def _():
                zero_tile(k).wait()

    xbuf[slot] = x_ref[...]
    src = xbuf.at[slot]
    for r in range(tm):
        _row_copy(src, r, xs_hbm, pos_ref[0, 0, r], sems.at[slot]).start(priority=r % 2)

    def wait_tile(sl):
        pltpu.make_async_copy(xbuf.at[sl], xs_hbm.at[pl.ds(0, tm)], sems.at[sl]).wait()

    @pl.when(i > 0)
    def _():
        wait_tile(1 - slot)

    @pl.when(i == pl.num_programs(0) - 1)
    def _():
        wait_tile(slot)


def _scatter(zero_tiles, pos3, rows, n_rows, te):
    n_tiles, _, tm = pos3.shape
    w = rows.shape[1]
    grid_spec = pltpu.PrefetchScalarGridSpec(
        num_scalar_prefetch=1,
        grid=(n_tiles,),
        in_specs=[pl.BlockSpec((1, 1, tm), lambda i, zt: (i, 0, 0), memory_space=pltpu.SMEM),
                  pl.BlockSpec((tm, w), lambda i, zt: (i, 0))],
        out_specs=pl.BlockSpec(memory_space=pl.ANY),
        scratch_shapes=[pltpu.VMEM((2, tm, w), rows.dtype), pltpu.VMEM((te, w), rows.dtype),
                        pltpu.SemaphoreType.DMA((2,)), pltpu.SemaphoreType.DMA(())],
    )
    return pl.pallas_call(
        functools.partial(_scatter_kernel, tm=tm, te=te),
        grid_spec=grid_spec,
        out_shape=jax.ShapeDtypeStruct((n_rows, w), rows.dtype),
        compiler_params=pltpu.CompilerParams(
            dimension_semantics=("arbitrary",), vmem_limit_bytes=VMEM_LIMIT),
        name="moe_scatter",
    )(zero_tiles, pos3, rows)


def _experts_kernel(lo_ref, hi_ref, nu_ref, xs_ref, wgu_lo_ref, wd_lo_ref, wgu_hi_ref, wd_hi_ref,
                    ys_ref):
    del lo_ref, hi_ref
    i = pl.program_id(0)
    d = ys_ref.shape[1]

    @pl.when(i < nu_ref[0])
    def _():
        x = xs_ref[:, :d].astype(bf16)
        wts = xs_ref[:, d:]

        def scaled_hidden(wgu_ref, col):
            gu = jnp.dot(x, wgu_ref[0], preferred_element_type=f32)
            g = gu[:, :EXPERT_FF]
            return ((g * _sigmoid(g)) * gu[:, EXPERT_FF:] * wts[:, col:col + 1]).astype(bf16)

        ys_ref[...] = (
            jnp.dot(scaled_hidden(wgu_lo_ref, 0), wd_lo_ref[0], preferred_element_type=f32)
            + jnp.dot(scaled_hidden(wgu_hi_ref, 1), wd_hi_ref[0], preferred_element_type=f32))

    @pl.when(i >= nu_ref[0])
    def _():
        ys_ref[...] = jnp.zeros_like(ys_ref)


def _experts(tile_lo, tile_hi, n_used, xs, wgu, wd, tm):
    n_rows, w = xs.shape
    d = wd.shape[2]
    up = lambda which: pl.BlockSpec((1, d, 2 * EXPERT_FF), lambda i, lo, hi, nu: ((lo, hi)[which][i], 0, 0))
    down = lambda which: pl.BlockSpec((1, EXPERT_FF, d), lambda i, lo, hi, nu: ((lo, hi)[which][i], 0, 0))
    grid_spec = pltpu.PrefetchScalarGridSpec(
        num_scalar_prefetch=3,
        grid=(n_rows // tm,),
        in_specs=[pl.BlockSpec((tm, w), lambda i, lo, hi, nu: (i, 0)), up(0), down(0), up(1), down(1)],
        out_specs=pl.BlockSpec((tm, d), lambda i, lo, hi, nu: (i, 0)),
    )
    return pl.pallas_call(
        _experts_kernel,
        grid_spec=grid_spec,
        out_shape=jax.ShapeDtypeStruct((n_rows, d), f32),
        compiler_params=pltpu.CompilerParams(
            dimension_semantics=("arbitrary",), vmem_limit_bytes=VMEM_LIMIT),
        name="moe_experts",
    )(tile_lo, tile_hi, n_used, xs, wgu, wd, wgu, wd)


def _combine_kernel(pos_ref, nxt_ref, ys_hbm, h_ref, nf_ref, o_ref, gbuf, sems, *, tm):
    i = pl.program_id(0)
    slot = i % 2

    def issue(p_ref, sl):
        for r in range(tm):
            _row_copy(ys_hbm, p_ref[0, 0, r], gbuf.at[sl], r, sems.at[sl]).start(priority=r % 2)

    @pl.when(i == 0)
    def _():
        issue(pos_ref, 0)

    @pl.when(i + 1 < pl.num_programs(0))
    def _():
        issue(nxt_ref, 1 - slot)

    pltpu.make_async_copy(ys_hbm.at[pl.ds(0, tm)], gbuf.at[slot], sems.at[slot]).wait()
    o_ref[...] = _rms(h_ref[...] + gbuf[slot], nf_ref[...])


def _combine(pos3, ys, h, nf):
    n_tiles, _, tm = pos3.shape
    t, d = h.shape
    row = pl.BlockSpec((tm, d), lambda i: (i, 0))
    last = n_tiles - 1
    return pl.pallas_call(
        functools.partial(_combine_kernel, tm=tm),
        grid=(n_tiles,),
        in_specs=[pl.BlockSpec((1, 1, tm), lambda i: (i, 0, 0), memory_space=pltpu.SMEM),
                  pl.BlockSpec((1, 1, tm), lambda i: (jnp.minimum(i + 1, last), 0, 0),
                               memory_space=pltpu.SMEM),
                  pl.BlockSpec(memory_space=pl.ANY), row, _const_spec((1, d))],
        out_specs=row,
        out_shape=jax.ShapeDtypeStruct((t, d), f32),
        scratch_shapes=[pltpu.VMEM((2, tm, d), f32), pltpu.SemaphoreType.DMA((2,))],
        compiler_params=pltpu.CompilerParams(
            dimension_semantics=("arbitrary",), vmem_limit_bytes=VMEM_LIMIT),
        name="moe_combine",
    )(pos3, pos3, ys, h, nf)


def _tile_positions(pos, tm):
    return pos.reshape(-1, 1, tm)


class _Tiles(NamedTuple):
    proj: int
    rows: int
    query: int
    expert: int
    scatter: int
    combine: int

    @classmethod
    def plan(cls, seq_len):
        cap = lambda n: min(n, seq_len)
        return cls(proj=cap(1024), rows=cap(512), query=cap(256), expert=cap(512),
                   scatter=cap(1024), combine=cap(512))


def kernel(x, meta, norm_mix, w_in, w_sb_o, conv_w, conv_b, conv_ln_g, conv_ln_b, w_conv_o, w_out,
           norm_ffn, w_router_group, b_router_group, w_router_expert, b_router_expert, w_gate, w_up,
           w_down, norm_final):
    assert norm_mix.shape[0] == 1, "single-layer block"
    b, s, d = x.shape
    t = b * s
    tiles_of = _Tiles.plan(s)
    tm, tq, te = tiles_of.rows, tiles_of.query, tiles_of.expert
    x2 = x.reshape(t, d)
    w_in_b = w_in[0].astype(bf16)
    g_mix = norm_mix[0][None]

    q, k, v, u, gs, gc = _in_proj(x2, g_mix, w_in_b, tiles_of.proj)
    _, km, vm, um, _, _ = _in_proj(meta, g_mix, w_in_b, N_META)
    pad_rows = lambda a, top, bottom: jnp.pad(a, ((top, bottom), (0, 0)))
    km = pad_rows(km, 0, LANES - N_META)
    vm = pad_rows(vm, 0, LANES - N_META)
    shp = (b, s, SB_WIDTH)
    att = _attention(q.reshape(shp), k.reshape(shp), v.reshape(shp), km, vm, tq).reshape(t, SB_WIDTH)

    mhalo = pad_rows(um, CONV_HALO - N_META, 0)
    wrt = jnp.concatenate([w_router_expert[0], w_router_group[0]], axis=1).T
    wrt = pad_rows(wrt, 0, ROUTER_ROWS - wrt.shape[0]).astype(bf16)
    brt = jnp.concatenate([b_router_expert[0], b_router_group[0]])[:, None]
    brt = pad_rows(brt, 0, ROUTER_ROWS - brt.shape[0]).astype(f32)
    h, xrow, route, counts = _mix(
        att, u, mhalo, gs, gc, x2,
        w_sb_o[0].astype(bf16), w_conv_o[0].astype(bf16), w_out[0].astype(bf16),
        conv_w[0], conv_b[0][None], conv_ln_g[0][None], conv_ln_b[0][None],
        norm_ffn[0][None], wrt, brt, tm, s)

    n_rows = t + N_CLASSES * te
    cnt = counts[:N_CLASSES, 0].astype(jnp.int32)
    tiles = (cnt + te - 1) // te
    tile_end = jnp.cumsum(tiles)
    row_start = (tile_end - tiles) * te
    classes = jnp.arange(N_CLASSES, dtype=jnp.int32)
    pos = jnp.sum(jnp.where(route[0][:, None] == classes, row_start, 0), axis=-1) + route[1]
    tile_ids = jnp.arange(n_rows // te, dtype=jnp.int32)
    tile_cls = jnp.minimum(jnp.sum(tile_ids[:, None] >= tile_end, axis=-1), N_CLASSES - 1)
    cls_hot = tile_cls[:, None] == classes
    lo_of = jnp.array([c // PAIRS_PER_GROUP * EXPERTS_PER_GROUP + GROUP_PAIRS[c % PAIRS_PER_GROUP][0]
                       for c in range(N_CLASSES)], jnp.int32)
    hi_of = jnp.array([c // PAIRS_PER_GROUP * EXPERTS_PER_GROUP + GROUP_PAIRS[c % PAIRS_PER_GROUP][1]
                       for c in range(N_CLASSES)], jnp.int32)
    tile_lo = jnp.sum(jnp.where(cls_hot, lo_of, 0), axis=-1).astype(jnp.int32)
    tile_hi = jnp.sum(jnp.where(cls_hot, hi_of, 0), axis=-1).astype(jnp.int32)
    n_used = tile_end[-1:].astype(jnp.int32)
    last_tile = jnp.where(tiles > 0, tile_end - 1, -1)
    tail = n_used + classes
    zero_tiles = jnp.concatenate([last_tile, jnp.where(tail < n_rows // te, tail, -1)]).astype(jnp.int32)

    xs = _scatter(zero_tiles, _tile_positions(pos, tiles_of.scatter), xrow, n_rows, te)
    wgu = jnp.concatenate([w_gate[0], w_up[0]], axis=-1).astype(bf16)
    ys = _experts(tile_lo, tile_hi, n_used, xs, wgu, w_down[0].astype(bf16), te)
    out = _combine(_tile_positions(pos, tiles_of.combine), ys, h, norm_final[None])
    return out.reshape(b, s, d)
```

```python
import functools
from typing import NamedTuple

import jax
import jax.numpy as jnp
import numpy as np
from jax import lax
from jax.experimental import pallas as pl
from jax.experimental.pallas import tpu as pltpu

N_META = 16
SB_HEAD_DIM = 64
SB_WIDTH = 512
CONV_WIDTH = 512
CONV_KERNEL = 31
N_GROUPS = 4
EXPERTS_PER_GROUP = 4
N_EXPERTS = N_GROUPS * EXPERTS_PER_GROUP
EXPERT_FF = 512
GROUP_PAIRS = [(a, b) for a in range(EXPERTS_PER_GROUP) for b in range(a + 1, EXPERTS_PER_GROUP)]
PAIRS_PER_GROUP = len(GROUP_PAIRS)
N_CLASSES = N_GROUPS * PAIRS_PER_GROUP
CLASS_ROWS = 32
EPS = 1e-6

LANES = 128
SUBLANES = 8
HEAD_PAIR = 2 * SB_HEAD_DIM
CONV_HALO = 32
ROUTER_ROWS = 32
STAY_SUM_EXIT = 150.0
LOG2_E = 1.4426950408889634
VMEM_LIMIT = 56 * 1024 * 1024

f32 = jnp.float32
bf16 = jnp.bfloat16


def _sigmoid(x):
    return 1.0 / (1.0 + jnp.exp(-x))


def _rms(x, g):
    return x * lax.rsqrt(jnp.mean(x * x, axis=-1, keepdims=True) + EPS) * g


def _const_spec(shape):
    return pl.BlockSpec(shape, lambda *_: (0,) * len(shape))


def _in_proj_kernel(x_ref, g_ref, w_ref, q_ref, k_ref, v_ref, u_ref, gs_ref, gc_ref):
    xb = _rms(x_ref[...], g_ref[...]).astype(bf16)

    def proj(lo, width):
        return jnp.dot(xb, w_ref[:, lo:lo + width], preferred_element_type=f32)

    o = 0
    q_ref[...] = (proj(o, SB_WIDTH) * (SB_HEAD_DIM ** -0.5 * LOG2_E)).astype(bf16); o += SB_WIDTH
    k_ref[...] = proj(o, SB_WIDTH).astype(bf16); o += SB_WIDTH
    v_ref[...] = proj(o, SB_WIDTH).astype(bf16); o += SB_WIDTH
    a = proj(o, CONV_WIDTH); o += CONV_WIDTH
    b = proj(o, CONV_WIDTH); o += CONV_WIDTH
    u_ref[...] = (a * _sigmoid(b)).astype(bf16)
    d = gs_ref.shape[1]
    gs_ref[...] = _sigmoid(proj(o, d)).astype(bf16); o += d
    gc_ref[...] = _sigmoid(proj(o, d)).astype(bf16)


def _in_proj(x2, g, w_b, tm):
    t, d = x2.shape
    n_in = w_b.shape[1]
    row = lambda w: pl.BlockSpec((tm, w), lambda i: (i, 0))
    sds = lambda w: jax.ShapeDtypeStruct((t, w), bf16)
    return pl.pallas_call(
        _in_proj_kernel,
        grid=(t // tm,),
        in_specs=[row(d), _const_spec((1, d)), _const_spec((d, n_in))],
        out_specs=[row(SB_WIDTH), row(SB_WIDTH), row(SB_WIDTH), row(CONV_WIDTH), row(d), row(d)],
        out_shape=[sds(SB_WIDTH), sds(SB_WIDTH), sds(SB_WIDTH), sds(CONV_WIDTH), sds(d), sds(d)],
        compiler_params=pltpu.CompilerParams(
            dimension_semantics=("arbitrary",), vmem_limit_bytes=VMEM_LIMIT),
        name="in_proj",
    )(x2, g, w_b)


def _attn_kernel(q_ref, k_ref, v_ref, km_ref, vm_ref, tri_ref, o_ref, acc_ref, c_ref, *, tq, n_pairs):
    i = pl.program_id(1)
    lane = lax.broadcasted_iota(jnp.int32, (1, LANES), 1)
    low = lane < SB_HEAD_DIM

    def visit(load_kv, valid):
        for p in range(n_pairs):
            cols = slice(p * HEAD_PAIR, (p + 1) * HEAD_PAIR)
            kt, vt = load_kv(cols)
            tk = kt.shape[0]
            tri = tri_ref[:tk, :tk]
            q2 = q_ref[0, :, cols]
            zero = jnp.zeros_like(q2)
            qq = jnp.concatenate([jnp.where(low, q2, zero), jnp.where(low, zero, q2)], axis=0)
            z = lax.dot_general(qq, kt, (((1,), (1,)), ((), ())),
                                preferred_element_type=f32)
            sp = jnp.maximum(z, 0.0) + jnp.log2(1.0 + jnp.exp2(-jnp.abs(z)))
            if valid is not None:
                sp = jnp.where(valid, sp, 0.0)
            spb = sp.astype(bf16)
            c = c_ref[p]
            after = jnp.tile(c, (1, tk // LANES)) + jnp.dot(spb, tri, preferred_element_type=f32)
            a = jnp.exp2((z - sp) - after)
            if valid is not None:
                a = jnp.where(valid, a, 0.0)
            pv = jnp.dot(a.astype(bf16), vt, preferred_element_type=f32)
            c_new = after[:, 0:1] + spb[:, 0:1].astype(f32)
            c_ref[p] = jnp.broadcast_to(c_new, c.shape)
            acc_ref[:, cols] += jnp.where(low, pv[:tq], pv[tq:])

    def seq_tile(j):
        off = pl.multiple_of(j * tq, tq)
        return lambda cols: (k_ref[0, pl.ds(off, tq), cols], v_ref[0, pl.ds(off, tq), cols])

    def more_needed():
        return jnp.min(c_ref[...]) < STAY_SUM_EXIT

    acc_ref[...] = jnp.zeros_like(acc_ref)
    c_ref[...] = jnp.zeros_like(c_ref)

    qrow = lax.broadcasted_iota(jnp.int32, (2 * tq, tq), 0) & (tq - 1)
    kcol = lax.broadcasted_iota(jnp.int32, (2 * tq, tq), 1)
    visit(seq_tile(i), kcol < qrow)

    visit(seq_tile(jnp.maximum(i - 1, 0)), jnp.broadcast_to(i > 0, (2 * tq, tq)))

    def cond(carry):
        j, more = carry
        return jnp.logical_and(j >= 0, more)

    def body(carry):
        j, _ = carry
        visit(seq_tile(j), None)
        return j - 1, more_needed()

    _, more = lax.while_loop(cond, body, (i - 2, more_needed()))

    @pl.when(more)
    def _():
        mcol = lax.broadcasted_iota(jnp.int32, (2 * tq, LANES), 1)
        visit(lambda cols: (km_ref[:, cols], vm_ref[:, cols]), mcol < N_META)

    o_ref[0] = acc_ref[...].astype(o_ref.dtype)


def _attention(q, k, v, km, vm, tq):
    b, s, w = q.shape
    n_pairs = w // HEAD_PAIR
    tri = jnp.asarray(np.tril(np.ones((tq, tq), np.float32), -1), bf16)
    seq = pl.BlockSpec((1, s, w), lambda bi, i: (bi, 0, 0))
    tile = pl.BlockSpec((1, tq, w), lambda bi, i: (bi, i, 0))
    return pl.pallas_call(
        functools.partial(_attn_kernel, tq=tq, n_pairs=n_pairs),
        grid=(b, s // tq),
        in_specs=[tile, seq, seq, _const_spec((LANES, w)), _const_spec((LANES, w)),
                  _const_spec((tq, tq))],
        out_specs=tile,
        out_shape=jax.ShapeDtypeStruct((b, s, w), bf16),
        scratch_shapes=[pltpu.VMEM((tq, w), f32), pltpu.VMEM((n_pairs, 2 * tq, LANES), f32)],
        compiler_params=pltpu.CompilerParams(
            dimension_semantics=("arbitrary", "arbitrary"), vmem_limit_bytes=VMEM_LIMIT),
        name="sb_attention",
    )(q, k, v, km, vm, tri)


def _mix_kernel(att_ref, u_ref, prev_ref, mhalo_ref, gs_ref, gc_ref, x_ref,
                wsb_ref, wcv_ref, wout_ref, cw_ref, cb_ref, lng_ref, lnb_ref, nf_ref,
                wrt_ref, brt_ref, upper_ref,
                h_ref, xn_ref, route_ref, cnt_out_ref,
                ubuf_ref, shift_ref, ycv_ref, cnt_ref, *, tm, tiles_per_seq, chunk):
    i = pl.program_id(0)
    first = (i % tiles_per_seq) == 0
    halo = jnp.where(first, mhalo_ref[...], prev_ref[...])
    ubuf_ref[0:CONV_HALO, :] = halo.astype(f32)
    ubuf_ref[CONV_HALO:, :] = u_ref[...].astype(f32)

    lead = CONV_HALO - (CONV_KERNEL - 1)
    span = shift_ref.shape[1]
    for ph in range(1, SUBLANES):
        shift_ref[ph - 1] = ubuf_ref[ph:ph + span, :]

    def tap(r0, kk):
        ph = (lead + kk) % SUBLANES
        base = r0 + lead + kk - ph
        if ph == 0:
            return ubuf_ref[base:base + chunk, :]
        return shift_ref[ph - 1, base:base + chunk, :]

    for r0 in range(0, tm, chunk):
        y = jnp.broadcast_to(cb_ref[...], (chunk, CONV_WIDTH))
        for kk in range(CONV_KERNEL):
            y = y + cw_ref[kk:kk + 1, :] * tap(r0, kk)
        mu = jnp.mean(y, axis=-1, keepdims=True)
        yc = y - mu
        var = jnp.mean(yc * yc, axis=-1, keepdims=True)
        yn = yc * lax.rsqrt(var + EPS) * lng_ref[...] + lnb_ref[...]
        ycv_ref[r0:r0 + chunk, :] = (yn * _sigmoid(yn)).astype(bf16)

    y_sb = jnp.dot(att_ref[...], wsb_ref[...], preferred_element_type=f32)
    y_cv = jnp.dot(ycv_ref[...], wcv_ref[...], preferred_element_type=f32)
    merged = gs_ref[...].astype(f32) * y_sb + gc_ref[...].astype(f32) * y_cv
    h = x_ref[...] + jnp.dot(merged.astype(bf16), wout_ref[...], preferred_element_type=f32)
    h_ref[...] = h
    xn = _rms(h, nf_ref[...])
    xnb = xn.astype(bf16)
    d_model = xn.shape[1]
    xn_ref[:, :d_model] = xn

    lt = lax.dot_general(wrt_ref[...], xnb, (((1,), (1,)), ((), ())),
                         preferred_element_type=f32) + brt_ref[...]
    gl = [lt[N_EXPERTS + g:N_EXPERTS + g + 1, :] for g in range(N_GROUPS)]
    gmax = functools.reduce(jnp.maximum, gl)
    gsum = functools.reduce(lambda a, b: a + b, [jnp.exp(x - gmax) for x in gl])
    g_p = 1.0 / gsum
    gidx = jnp.full_like(gmax, N_GROUPS - 1)
    for g in range(N_GROUPS - 2, -1, -1):
        gidx = jnp.where(gl[g] == gmax, float(g), gidx)
    el = []
    for j in range(EXPERTS_PER_GROUP):
        x = lt[(N_GROUPS - 1) * EXPERTS_PER_GROUP + j:(N_GROUPS - 1) * EXPERTS_PER_GROUP + j + 1, :]
        for g in range(N_GROUPS - 2, -1, -1):
            x = jnp.where(gidx == float(g), lt[g * EXPERTS_PER_GROUP + j:g * EXPERTS_PER_GROUP + j + 1, :], x)
        el.append(x)
    m1 = functools.reduce(jnp.maximum, el)
    i1 = jnp.full_like(m1, EXPERTS_PER_GROUP - 1)
    for j in range(EXPERTS_PER_GROUP - 2, -1, -1):
        i1 = jnp.where(el[j] == m1, float(j), i1)
    el2 = [jnp.where(i1 == float(j), -jnp.inf, el[j]) for j in range(EXPERTS_PER_GROUP)]
    m2 = functools.reduce(jnp.maximum, el2)
    i2 = jnp.full_like(m2, EXPERTS_PER_GROUP - 1)
    for j in range(EXPERTS_PER_GROUP - 2, -1, -1):
        i2 = jnp.where(el2[j] == m2, float(j), i2)
    w1 = g_p / (1.0 + jnp.exp(m2 - m1))
    w2 = g_p - w1
    lo = jnp.minimum(i1, i2)
    hi = jnp.maximum(i1, i2)
    pair = jnp.where(lo == 0.0, hi - 1.0, jnp.where(lo == 1.0, hi + 1.0, float(PAIRS_PER_GROUP - 1)))
    cls = gidx * PAIRS_PER_GROUP + pair
    first_is_lo = i1 < i2
    w_lo = jnp.where(first_is_lo, w1, w2)
    w_hi = jnp.where(first_is_lo, w2, w1)

    @pl.when(i == 0)
    def _():
        cnt_ref[...] = jnp.zeros_like(cnt_ref)

    in_cls = lax.broadcasted_iota(jnp.int32, (CLASS_ROWS, tm), 0).astype(f32) == cls
    member = jnp.where(in_cls, 1.0, 0.0).astype(bf16)
    before = jnp.dot(member, upper_ref[...], preferred_element_type=f32)
    cnt = cnt_ref[...]
    rank = jnp.sum(jnp.where(in_cls, before + jnp.tile(cnt, (1, tm // LANES)), 0.0),
                   axis=0, keepdims=True)
    cnt_ref[...] = cnt + jnp.dot(member, jnp.ones((tm, LANES), bf16), preferred_element_type=f32)
    cnt_out_ref[...] = cnt_ref[...]
    pad = jnp.zeros((6, tm), f32)
    route_ref[...] = jnp.concatenate([cls, rank, pad], axis=0).astype(jnp.int32)
    wrow = lax.broadcasted_iota(jnp.int32, (LANES, tm), 0)
    w_t = jnp.where(wrow == 0, w_lo, jnp.where(wrow == 1, w_hi, 0.0))
    xn_ref[:, d_model:] = w_t.T


def _mix(att, u, mhalo, gs, gc, x2, wsb, wcv, wout, cw, cb, lng, lnb, nf, wrt, brt, tm, seq_len):
    t, d = x2.shape
    tiles_per_seq = seq_len // tm
    row = lambda w: pl.BlockSpec((tm, w), lambda i: (i, 0))
    halo_blocks = tm // CONV_HALO
    prev = pl.BlockSpec((CONV_HALO, CONV_WIDTH), lambda i: (jnp.maximum(i * halo_blocks - 1, 0), 0))
    kern = functools.partial(_mix_kernel, tm=tm, tiles_per_seq=tiles_per_seq, chunk=64)
    upper = jnp.asarray(np.triu(np.ones((tm, tm), np.float32), 1), bf16)
    return pl.pallas_call(
        kern,
        grid=(t // tm,),
        in_specs=[row(SB_WIDTH), row(CONV_WIDTH), prev, _const_spec((CONV_HALO, CONV_WIDTH)),
                  row(d), row(d), row(d),
                  _const_spec((SB_WIDTH, d)), _const_spec((CONV_WIDTH, d)), _const_spec((d, d)),
                  _const_spec((CONV_KERNEL, CONV_WIDTH)), _const_spec((1, CONV_WIDTH)),
                  _const_spec((1, CONV_WIDTH)), _const_spec((1, CONV_WIDTH)), _const_spec((1, d)),
                  _const_spec((ROUTER_ROWS, d)), _const_spec((ROUTER_ROWS, 1)), _const_spec((tm, tm))],
        out_specs=[row(d), row(d + LANES), pl.BlockSpec((8, tm), lambda i: (0, i)),
                   _const_spec((CLASS_ROWS, LANES))],
        out_shape=[jax.ShapeDtypeStruct((t, d), f32), jax.ShapeDtypeStruct((t, d + LANES), f32),
                   jax.ShapeDtypeStruct((8, t), jnp.int32),
                   jax.ShapeDtypeStruct((CLASS_ROWS, LANES), f32)],
        scratch_shapes=[pltpu.VMEM((tm + CONV_HALO, CONV_WIDTH), f32),
                        pltpu.VMEM((SUBLANES - 1, tm + CONV_HALO - SUBLANES, CONV_WIDTH), f32),
                        pltpu.VMEM((tm, CONV_WIDTH), bf16),
                        pltpu.VMEM((CLASS_ROWS, LANES), f32)],
        compiler_params=pltpu.CompilerParams(
            dimension_semantics=("arbitrary",), vmem_limit_bytes=VMEM_LIMIT),
        name="mix",
    )(att, u, u, mhalo, gs, gc, x2, wsb, wcv, wout, cw, cb, lng, lnb, nf, wrt, brt, upper)


def _row_copy(src, src_row, dst, dst_row, sem):
    return pltpu.make_async_copy(src.at[pl.ds(src_row, 1)], dst.at[pl.ds(dst_row, 1)], sem)


def _scatter_kernel(zt_ref, pos_ref, x_ref, xs_hbm, xbuf, zbuf, sems, zsem, *, tm, te):
    i = pl.program_id(0)
    slot = i % 2

    @pl.when(i == 0)
    def _():
        zbuf[...] = jnp.zeros_like(zbuf)

        def zero_tile(k):
            start = pl.multiple_of(zt_ref[k] * te, te)
            return pltpu.make_async_copy(zbuf, xs_hbm.at[pl.ds(start, te)], zsem)

        for k in range(zt_ref.shape[0]):
            @pl.when(zt_ref[k] >= 0)
            def _():
                zero_tile(k).start()
        for k in range(zt_ref.shape[0]):
            @pl.when(zt_ref[k] >= 0)
            def _():
                zero_tile(k).wait()

    xbuf[slot] = x_ref[...]
    src = xbuf.at[slot]
    for r in range(tm):
        _row_copy(src, r, xs_hbm, pos_ref[0, 0, r], sems.at[slot]).start(priority=r % 2)

    def wait_tile(sl):
        pltpu.make_async_copy(xbuf.at[sl], xs_hbm.at[pl.ds(0, tm)], sems.at[sl]).wait()

    @pl.when(i > 0)
    def _():
        wait_tile(1 - slot)

    @pl.when(i == pl.num_programs(0) - 1)
    def _():
        wait_tile(slot)


def _scatter(zero_tiles, pos3, rows, n_rows, te):
    n_tiles, _, tm = pos3.shape
    w = rows.shape[1]
    grid_spec = pltpu.PrefetchScalarGridSpec(
        num_scalar_prefetch=1,
        grid=(n_tiles,),
        in_specs=[pl.BlockSpec((1, 1, tm), lambda i, zt: (i, 0, 0), memory_space=pltpu.SMEM),
                  pl.BlockSpec((tm, w), lambda i, zt: (i, 0))],
        out_specs=pl.BlockSpec(memory_space=pl.ANY),
        scratch_shapes=[pltpu.VMEM((2, tm, w), rows.dtype), pltpu.VMEM((te, w), rows.dtype),
                        pltpu.SemaphoreType.DMA((2,)), pltpu.SemaphoreType.DMA(())],
    )
    return pl.pallas_call(
        functools.partial(_scatter_kernel, tm=tm, te=te),
        grid_spec=grid_spec,
        out_shape=jax.ShapeDtypeStruct((n_rows, w), rows.dtype),
        compiler_params=pltpu.CompilerParams(
            dimension_semantics=("arbitrary",), vmem_limit_bytes=VMEM_LIMIT),
        name="moe_scatter",
    )(zero_tiles, pos3, rows)


def _experts_kernel(lo_ref, hi_ref, nu_ref, xs_ref, wgu_lo_ref, wd_lo_ref, wgu_hi_ref, wd_hi_ref,
                    ys_ref):
    del lo_ref, hi_ref
    i = pl.program_id(0)
    d = ys_ref.shape[1]

    @pl.when(i < nu_ref[0])
    def _():
        x = xs_ref[:, :d].astype(bf16)
        wts = xs_ref[:, d:]

        def scaled_hidden(wgu_ref, col):
            gu = jnp.dot(x, wgu_ref[0], preferred_element_type=f32)
            g = gu[:, :EXPERT_FF]
            return ((g * _sigmoid(g)) * gu[:, EXPERT_FF:] * wts[:, col:col + 1]).astype(bf16)

        ys_ref[...] = (
            jnp.dot(scaled_hidden(wgu_lo_ref, 0), wd_lo_ref[0], preferred_element_type=f32)
            + jnp.dot(scaled_hidden(wgu_hi_ref, 1), wd_hi_ref[0], preferred_element_type=f32))

    @pl.when(i >= nu_ref[0])
    def _():
        ys_ref[...] = jnp.zeros_like(ys_ref)


def _experts(tile_lo, tile_hi, n_used, xs, wgu, wd, tm):
    n_rows, w = xs.shape
    d = wd.shape[2]
    up = lambda which: pl.BlockSpec((1, d, 2 * EXPERT_FF), lambda i, lo, hi, nu: ((lo, hi)[which][i], 0, 0))
    down = lambda which: pl.BlockSpec((1, EXPERT_FF, d), lambda i, lo, hi, nu: ((lo, hi)[which][i], 0, 0))
    grid_spec = pltpu.PrefetchScalarGridSpec(
        num_scalar_prefetch=3,
        grid=(n_rows // tm,),
        in_specs=[pl.BlockSpec((tm, w), lambda i, lo, hi, nu: (i, 0)), up(0), down(0), up(1), down(1)],
        out_specs=pl.BlockSpec((tm, d), lambda i, lo, hi, nu: (i, 0)),
    )
    return pl.pallas_call(
        _experts_kernel,
        grid_spec=grid_spec,
        out_shape=jax.ShapeDtypeStruct((n_rows, d), f32),
        compiler_params=pltpu.CompilerParams(
            dimension_semantics=("arbitrary",), vmem_limit_bytes=VMEM_LIMIT),
        name="moe_experts",
    )(tile_lo, tile_hi, n_used, xs, wgu, wd, wgu, wd)


def _combine_kernel(pos_ref, nxt_ref, ys_hbm, h_ref, nf_ref, o_ref, gbuf, sems, *, tm):
    i = pl.program_id(0)
    slot = i % 2

    def issue(p_ref, sl):
        for r in range(tm):
            _row_copy(ys_hbm, p_ref[0, 0, r], gbuf.at[sl], r, sems.at[sl]).start(priority=r % 2)

    @pl.when(i == 0)
    def _():
        issue(pos_ref, 0)

    @pl.when(i + 1 < pl.num_programs(0))
    def _():
        issue(nxt_ref, 1 - slot)

    pltpu.make_async_copy(ys_hbm.at[pl.ds(0, tm)], gbuf.at[slot], sems.at[slot]).wait()
    o_ref[...] = _rms(h_ref[...] + gbuf[slot], nf_ref[...])


def _combine(pos3, ys, h, nf):
    n_tiles, _, tm = pos3.shape
    t, d = h.shape
    row = pl.BlockSpec((tm, d), lambda i: (i, 0))
    last = n_tiles - 1
    return pl.pallas_call(
        functools.partial(_combine_kernel, tm=tm),
        grid=(n_tiles,),
        in_specs=[pl.BlockSpec((1, 1, tm), lambda i: (i, 0, 0), memory_space=pltpu.SMEM),
                  pl.BlockSpec((1, 1, tm), lambda i: (jnp.minimum(i + 1, last), 0, 0),
                               memory_space=pltpu.SMEM),
                  pl.BlockSpec(memory_space=pl.ANY), row, _const_spec((1, d))],
        out_specs=row,
        out_shape=jax.ShapeDtypeStruct((t, d), f32),
        scratch_shapes=[pltpu.VMEM((2, tm, d), f32), pltpu.SemaphoreType.DMA((2,))],
        compiler_params=pltpu.CompilerParams(
            dimension_semantics=("arbitrary",), vmem_limit_bytes=VMEM_LIMIT),
        name="moe_combine",
    )(pos3, pos3, ys, h, nf)


def _tile_positions(pos, tm):
    return pos.reshape(-1, 1, tm)


class _Tiles(NamedTuple):
    proj: int
    rows: int
    query: int
    expert: int
    scatter: int
    combine: int

    @classmethod
    def plan(cls, seq_len):
        cap = lambda n: min(n, seq_len)
        return cls(proj=cap(1024), rows=cap(512), query=cap(256), expert=cap(512),
                   scatter=cap(1024), combine=cap(512))


def kernel(x, meta, norm_mix, w_in, w_sb_o, conv_w, conv_b, conv_ln_g, conv_ln_b, w_conv_o, w_out,
           norm_ffn, w_router_group, b_router_group, w_router_expert, b_router_expert, w_gate, w_up,
           w_down, norm_final):
    assert norm_mix.shape[0] == 1, "single-layer block"
    b, s, d = x.shape
    t = b * s
    tiles_of = _Tiles.plan(s)
    tm, tq, te = tiles_of.rows, tiles_of.query, tiles_of.expert
    x2 = x.reshape(t, d)
    w_in_b = w_in[0].astype(bf16)
    g_mix = norm_mix[0][None]

    q, k, v, u, gs, gc = _in_proj(x2, g_mix, w_in_b, tiles_of.proj)
    _, km, vm, um, _, _ = _in_proj(meta, g_mix, w_in_b, N_META)
    pad_rows = lambda a, top, bottom: jnp.pad(a, ((top, bottom), (0, 0)))
    km = pad_rows(km, 0, LANES - N_META)
    vm = pad_rows(vm, 0, LANES - N_META)
    shp = (b, s, SB_WIDTH)
    att = _attention(q.reshape(shp), k.reshape(shp), v.reshape(shp), km, vm, tq).reshape(t, SB_WIDTH)

    mhalo = pad_rows(um, CONV_HALO - N_META, 0)
    wrt = jnp.concatenate([w_router_expert[0], w_router_group[0]], axis=1).T
    wrt = pad_rows(wrt, 0, ROUTER_ROWS - wrt.shape[0]).astype(bf16)
    brt = jnp.concatenate([b_router_expert[0], b_router_group[0]])[:, None]
    brt = pad_rows(brt, 0, ROUTER_ROWS - brt.shape[0]).astype(f32)
    h, xrow, route, counts = _mix(
        att, u, mhalo, gs, gc, x2,
        w_sb_o[0].astype(bf16), w_conv_o[0].astype(bf16), w_out[0].astype(bf16),
        conv_w[0], conv_b[0][None], conv_ln_g[0][None], conv_ln_b[0][None],
        norm_ffn[0][None], wrt, brt, tm, s)

    n_rows = t + N_CLASSES * te
    cnt = counts[:N_CLASSES, 0].astype(jnp.int32)
    tiles = (cnt + te - 1) // te
    tile_end = jnp.cumsum(tiles)
    row_start = (tile_end - tiles) * te
    classes = jnp.arange(N_CLASSES, dtype=jnp.int32)
    pos = jnp.sum(jnp.where(route[0][:, None] == classes, row_start, 0), axis=-1) + route[1]
    tile_ids = jnp.arange(n_rows // te, dtype=jnp.int32)
    tile_cls = jnp.minimum(jnp.sum(tile_ids[:, None] >= tile_end, axis=-1), N_CLASSES - 1)
    cls_hot = tile_cls[:, None] == classes
    lo_of = jnp.array([c // PAIRS_PER_GROUP * EXPERTS_PER_GROUP + GROUP_PAIRS[c % PAIRS_PER_GROUP][0]
                       for c in range(N_CLASSES)], jnp.int32)
    hi_of = jnp.array([c // PAIRS_PER_GROUP * EXPERTS_PER_GROUP + GROUP_PAIRS[c % PAIRS_PER_GROUP][1]
                       for c in range(N_CLASSES)], jnp.int32)
    tile_lo = jnp.sum(jnp.where(cls_hot, lo_of, 0), axis=-1).astype(jnp.int32)
    tile_hi = jnp.sum(jnp.where(cls_hot, hi_of, 0), axis=-1).astype(jnp.int32)
    n_used = tile_end[-1:].astype(jnp.int32)
    last_tile = jnp.where(tiles > 0, tile_end - 1, -1)
    tail = n_used + classes
    zero_tiles = jnp.concatenate([last_tile, jnp.where(tail < n_rows // te, tail, -1)]).astype(jnp.int32)

    xs = _scatter(zero_tiles, _tile_positions(pos, tiles_of.scatter), xrow, n_rows, te)
    wgu = jnp.concatenate([w_gate[0], w_up[0]], axis=-1).astype(bf16)
    ys = _experts(tile_lo, tile_hi, n_used, xs, wgu, w_down[0].astype(bf16), te)
    out = _combine(_tile_positions(pos, tiles_of.combine), ys, h, norm_final[None])
    return out.reshape(b, s, d)
```
